```python
import jax, jax.numpy as jnp
from jax import lax
import numpy as np

D_MODEL = 1024
BATCH = 2
SEQ = 8192
DEPTH = 1

MLA_HEADS = 8
MLA_Q_RANK = 256
MLA_KV_RANK = 128
MLA_NOPE_DIM = 64
MLA_ROPE_DIM = 32
MLA_V_DIM = 64
MLA_QK_DIM = MLA_NOPE_DIM + MLA_ROPE_DIM
Q_BLOCK = 128
RET_HEADS = 8
RET_QK_DIM = D_MODEL // (2 * RET_HEADS)
RET_V_DIM = 2 * RET_QK_DIM
RET_CHUNK = 128
FFN_HIDDEN = -(-8 * D_MODEL // (3 * 256)) * 256
ROPE_THETA = 10000.0
EPS = 1e-6

IN_SPLITS = [
    MLA_Q_RANK,
    MLA_KV_RANK,
    MLA_ROPE_DIM,
    RET_HEADS * RET_QK_DIM,
    RET_HEADS * RET_QK_DIM,
    RET_HEADS * RET_V_DIM,
    RET_HEADS * RET_V_DIM,
    2 * D_MODEL,
]
IN_WIDTH = sum(IN_SPLITS)

kernel_name = "hybrid_mla_retention_gated_block"


def _rms(xf):
    return xf * lax.rsqrt(jnp.mean(xf * xf, axis=-1, keepdims=True) + EPS)


def rms_norm(x, g):
    y = _rms(x.astype(jnp.float32)) * g.astype(jnp.float32)
    return y.astype(x.dtype)


def rope(x, positions):
    half = x.shape[-1] // 2
    inv = ROPE_THETA ** (-jnp.arange(half, dtype=jnp.float32) / half)
    ang = positions.astype(jnp.float32)[..., None] * inv
    cos = jnp.cos(ang)[:, :, None, :]
    sin = jnp.sin(ang)[:, :, None, :]
    xf = x.astype(jnp.float32)
    x1, x2 = xf[..., :half], xf[..., half:]
    out = jnp.concatenate([x1 * cos - x2 * sin, x2 * cos + x1 * sin], axis=-1)
    return out.astype(x.dtype)


def mla_attention(c_q, c_kv, k_rope, positions, g_q_a, w_q_b, g_kv_a, w_kv_b, g_qn, g_kn):
    B, S, _ = c_q.shape
    H = MLA_HEADS
    q = (rms_norm(c_q, g_q_a) @ w_q_b).reshape(B, S, H, MLA_QK_DIM)
    kv = (rms_norm(c_kv, g_kv_a) @ w_kv_b).reshape(B, S, H, MLA_NOPE_DIM + MLA_V_DIM)
    k_nope, v = kv[..., :MLA_NOPE_DIM], kv[..., MLA_NOPE_DIM:]
    k_r = jnp.broadcast_to(k_rope[:, :, None, :], (B, S, H, MLA_ROPE_DIM))
    k = jnp.concatenate([k_nope, k_r], axis=-1)
    q = rms_norm(q, g_qn)
    k = rms_norm(k, g_kn)
    q = jnp.concatenate([q[..., :MLA_NOPE_DIM], rope(q[..., MLA_NOPE_DIM:], positions)], axis=-1)
    k = jnp.concatenate([k[..., :MLA_NOPE_DIM], rope(k[..., MLA_NOPE_DIM:], positions)], axis=-1)
    q = q.astype(jnp.float32).transpose(0, 2, 1, 3)
    k = k.astype(jnp.float32).transpose(0, 2, 1, 3)
    v = v.astype(jnp.float32).transpose(0, 2, 1, 3)
    scale = MLA_QK_DIM ** -0.5
    nb = S // Q_BLOCK
    qb = q.reshape(B, H, nb, Q_BLOCK, MLA_QK_DIM).transpose(2, 0, 1, 3, 4)

    def attend(q_blk):
        s = jnp.einsum('bhqd,bhkd->bhqk', q_blk, k) * scale
        p = jax.nn.softmax(s, axis=-1)
        return jnp.einsum('bhqk,bhkv->bhqv', p, v)

    o = lax.map(attend, qb)
    o = o.transpose(1, 0, 3, 2, 4).reshape(B, S, H * MLA_V_DIM)
    return o


def retention_dir(q, k, v, log_gamma, strict):
    B, H, S, dk = q.shape
    dv = v.shape[-1]
    C = RET_CHUNK
    n = S // C
    idx = jnp.arange(C, dtype=jnp.float32)
    diff = idx[:, None] - idx[None, :]
    mask = diff > 0 if strict else diff >= 0
    decay_in = jnp.where(mask, jnp.exp(log_gamma[:, None, None] * jnp.maximum(diff, 0.0)), 0.0)
    q_decay = jnp.exp(log_gamma[:, None] * (idx + 1.0))[..., None]
    k_decay = jnp.exp(log_gamma[:, None] * (C - 1.0 - idx))[..., None]
    chunk_decay = jnp.exp(log_gamma * C)[:, None, None]

    def to_chunks(a):
        return a.reshape(B, H, n, C, a.shape[-1]).transpose(2, 0, 1, 3, 4)

    def step(state, inp):
        qi, ki, vi = inp
        inner = jnp.einsum('bhcd,bhed->bhce', qi, ki) * decay_in
        inner = jnp.einsum('bhce,bhev->bhcv', inner, vi)
        cross = jnp.einsum('bhcd,bhdv->bhcv', qi * q_decay, state)
        new_state = state * chunk_decay + jnp.einsum('bhcd,bhcv->bhdv', ki * k_decay, vi)
        return new_state, inner + cross

    state0 = jnp.zeros((B, H, dk, dv), jnp.float32)
    _, out = lax.scan(step, state0, (to_chunks(q), to_chunks(k), to_chunks(v)))
    return out.transpose(1, 2, 0, 3, 4).reshape(B, H, S, dv)


def bidirectional_retention(q, k, v, decay_fwd, decay_bwd):
    lg_f = -jnp.exp(decay_fwd.astype(jnp.float32))
    lg_b = -jnp.exp(decay_bwd.astype(jnp.float32))
    fwd = retention_dir(q, k, v, lg_f, False)
    flip = lambda a: jnp.flip(a, axis=2)
    bwd = flip(retention_dir(flip(q), flip(k), flip(v), lg_b, True))
    return fwd + bwd


def setup_inputs(seed: int = 0) -> dict:
    key = jax.random.key(seed)
    ks = jax.random.split(key, 20)
    f32 = jnp.float32

    def w(k, fan_in, fan_out):
        return jax.random.normal(k, (fan_in, fan_out), f32) * fan_in ** -0.5

    def gain(k, n):
        return 1.0 + 0.02 * jax.random.normal(k, (n,), f32)

    gamma0 = 1.0 - 2.0 ** (-5.0 - jnp.arange(RET_HEADS, dtype=f32))
    decay_base = jnp.log(-jnp.log(gamma0))
    x = jax.random.normal(ks[0], (BATCH, SEQ, D_MODEL), f32)
    positions = (jnp.arange(SEQ, dtype=jnp.int32)[None, :]
                 + jax.random.randint(ks[1], (BATCH, 1), 0, SEQ, dtype=jnp.int32))
    return {
        "x": x,
        "positions": positions,
        "g_mix": gain(ks[2], D_MODEL),
        "w_in": w(ks[3], D_MODEL, IN_WIDTH),
        "g_q_a": gain(ks[4], MLA_Q_RANK),
        "w_q_b": w(ks[5], MLA_Q_RANK, MLA_HEADS * MLA_QK_DIM),
        "g_kv_a": gain(ks[6], MLA_KV_RANK),
        "w_kv_b": w(ks[7], MLA_KV_RANK, MLA_HEADS * (MLA_NOPE_DIM + MLA_V_DIM)),
        "g_qn": gain(ks[8], MLA_QK_DIM),
        "g_kn": gain(ks[9], MLA_QK_DIM),
        "w_mla_out": w(ks[10], MLA_HEADS * MLA_V_DIM, D_MODEL),
        "ret_decay_fwd": decay_base + 0.05 * jax.random.normal(ks[11], (RET_HEADS,), f32),
        "ret_decay_bwd": decay_base + 0.05 * jax.random.normal(ks[12], (RET_HEADS,), f32),
        "w_ret_out": w(ks[13], RET_HEADS * RET_V_DIM, D_MODEL),
        "w_out": w(ks[14], D_MODEL, D_MODEL),
        "g_ffn": gain(ks[15], D_MODEL),
        "w_gate_up": w(ks[16], D_MODEL, 2 * FFN_HIDDEN),
        "w_down": w(ks[17], FFN_HIDDEN, D_MODEL),
    }


def reference(x, positions, g_mix, w_in, g_q_a, w_q_b, g_kv_a, w_kv_b, g_qn, g_kn,
              w_mla_out, ret_decay_fwd, ret_decay_bwd, w_ret_out, w_out,
              g_ffn, w_gate_up, w_down):
    B, S, D = x.shape
    split_idx = np.cumsum(IN_SPLITS)[:-1].tolist()
    for _ in range(DEPTH):
        h = rms_norm(x, g_mix)
        proj = h @ w_in
        c_q, c_kv, k_rope, q_r, k_r, v_r, g_r, gate_logits = jnp.split(proj, split_idx, axis=-1)

        o_a = mla_attention(c_q, c_kv, k_rope, positions, g_q_a, w_q_b, g_kv_a, w_kv_b, g_qn, g_kn)
        y_a = o_a.astype(x.dtype) @ w_mla_out

        q_r = rope(q_r.reshape(B, S, RET_HEADS, RET_QK_DIM), positions)
        k_r = rope(k_r.reshape(B, S, RET_HEADS, RET_QK_DIM), positions)
        q_r = q_r.astype(jnp.float32).transpose(0, 2, 1, 3)
        k_r = k_r.astype(jnp.float32).transpose(0, 2, 1, 3) * (RET_QK_DIM ** -0.5)
        v_r = v_r.reshape(B, S, RET_HEADS, RET_V_DIM).astype(jnp.float32).transpose(0, 2, 1, 3)
        ret = bidirectional_retention(q_r, k_r, v_r, ret_decay_fwd, ret_decay_bwd)
        ret = _rms(ret).transpose(0, 2, 1, 3).reshape(B, S, RET_HEADS * RET_V_DIM)
        o_b = (jax.nn.silu(g_r.astype(jnp.float32)) * ret).astype(x.dtype)
        y_b = o_b @ w_ret_out

        gates = jax.nn.sigmoid(gate_logits.astype(jnp.float32))
        merged = gates[..., :D] * y_a.astype(jnp.float32) + gates[..., D:] * y_b.astype(jnp.float32)
        x = x + merged.astype(x.dtype) @ w_out

        h2 = rms_norm(x, g_ffn)
        gu = h2 @ w_gate_up
        gate, up = gu[..., :FFN_HIDDEN], gu[..., FFN_HIDDEN:]
        x = x + (jax.nn.silu(gate) * up) @ w_down
    return x
```

```python
import functools
import math

import numpy as np
import jax
import jax.numpy as jnp
from jax import lax
from jax.experimental import pallas as pl
from jax.experimental.pallas import tpu as pltpu

F32 = jnp.float32
BF16 = jnp.bfloat16

LANES = 128
VMEM_LIMIT_BYTES = 56 * 1024 * 1024

D_MODEL = 1024
MLA_HEADS = 8
MLA_Q_RANK = 256
MLA_KV_RANK = 128
MLA_NOPE = 64
MLA_ROPE = 32
MLA_V = 64
MLA_QK = MLA_NOPE + MLA_ROPE
RET_HEADS = 8
RET_QK = 64
RET_V = 128
FFN_HIDDEN = 2816
ROPE_THETA = 10000.0
EPS = 1e-6

OFF_SMALL = 0
OFF_QK_R = 512
OFF_V_R = OFF_QK_R + 2 * RET_HEADS * RET_QK
OFF_G_R = OFF_V_R + RET_HEADS * RET_V
OFF_GATE = OFF_G_R + RET_HEADS * RET_V
IN_WIDTH_PADDED = OFF_GATE + 2 * D_MODEL

TM_PROJ = 256
TM_OUT = 256
TQ = 256
TK = 256
RET_C = 256
FFN_CHUNK = 256


def _dot(a, b):
    return jnp.dot(a, b, preferred_element_type=F32)


def _rms_scale(v, n):
    return lax.rsqrt(jnp.sum(v * v, axis=-1, keepdims=True) * (1.0 / n) + EPS)


def _const_spec(shape):
    zeros = (0,) * len(shape)
    return pl.BlockSpec(shape, lambda *_: zeros, pipeline_mode=pl.Buffered(1))


def _proj_kernel(x_ref, pos_ref, gmix_ref, win_ref, gqa_ref, wqb_ref, gkva_ref, wkvb_ref,
                 gqn_ref, gknn_ref, gknr_ref, invm_ref, invr_ref,
                 q_ref, k_ref, vt_ref, qr_ref, kr_ref, vr_ref, sg_ref, gate_ref):
    x = x_ref[...]
    h = (x * _rms_scale(x, D_MODEL) * gmix_ref[...]).astype(BF16)

    pos = pos_ref[...].astype(F32)
    lane = lax.broadcasted_iota(jnp.int32, (1, LANES), 1)
    ang = pos * invm_ref[...]
    cs_m = jnp.cos(ang)
    sn = jnp.sin(ang)
    lo_m = jnp.where(lane < MLA_NOPE + MLA_ROPE // 2, -sn, 0.0)
    hi_m = jnp.where(lane >= MLA_NOPE + MLA_ROPE // 2, sn, 0.0)
    ang = pos * invr_ref[...]
    cs_r = jnp.cos(ang)
    sn = jnp.sin(ang)
    first_half = (lane % RET_QK) < RET_QK // 2
    lo_r = jnp.where(first_half, -sn, 0.0)
    hi_r = jnp.where(first_half, 0.0, sn)

    def rope_m(v):
        return (v * cs_m + pltpu.roll(v, LANES - MLA_ROPE // 2, 1) * lo_m
                + pltpu.roll(v, MLA_ROPE // 2, 1) * hi_m)

    def rope_r(v):
        return (v * cs_r + pltpu.roll(v, LANES - RET_QK // 2, 1) * lo_r
                + pltpu.roll(v, RET_QK // 2, 1) * hi_r)

    small = _dot(h, win_ref[:, OFF_SMALL:OFF_QK_R])
    cq = small[:, :MLA_Q_RANK]
    ckv = small[:, MLA_Q_RANK:MLA_Q_RANK + MLA_KV_RANK]
    krb = small[:, MLA_Q_RANK + MLA_KV_RANK:]

    cqn = (cq * _rms_scale(cq, MLA_Q_RANK) * gqa_ref[...]).astype(BF16)
    qf = _dot(cqn, wqb_ref[...])
    gqn = gqn_ref[...]
    for hd in range(MLA_HEADS):
        blk = qf[:, hd * LANES:(hd + 1) * LANES]
        qn = blk * _rms_scale(blk, MLA_QK) * gqn
        q_ref[:, hd * LANES:(hd + 1) * LANES] = rope_m(qn).astype(BF16)

    ckvn = (ckv * _rms_scale(ckv, MLA_KV_RANK) * gkva_ref[...]).astype(BF16)
    kvf = _dot(ckvn, wkvb_ref[...])
    kr_roped = rope_m(krb * gknr_ref[...])
    ss_rope = jnp.sum(krb * krb, axis=-1, keepdims=True)
    gknn = gknn_ref[...]
    for hd in range(MLA_HEADS):
        blk = kvf[:, hd * LANES:(hd + 1) * LANES]
        ss = jnp.sum(blk * blk, axis=-1, keepdims=True) + ss_rope
        r = lax.rsqrt(ss * (1.0 / MLA_QK) + EPS)
        k_ref[:, hd * LANES:(hd + 1) * LANES] = ((blk * gknn + kr_roped) * r).astype(BF16)
    vt_ref[...] = kvf[:, MLA_HEADS * LANES:].T.astype(BF16)

    qk_r = _dot(h, win_ref[:, OFF_QK_R:OFF_V_R])
    half = RET_HEADS * RET_QK
    for j in range(half // LANES):
        qr_ref[:, j * LANES:(j + 1) * LANES] = rope_r(qk_r[:, j * LANES:(j + 1) * LANES]).astype(BF16)
        kb = qk_r[:, half + j * LANES:half + (j + 1) * LANES]
        kr_ref[:, j * LANES:(j + 1) * LANES] = (rope_r(kb) * (RET_QK ** -0.5)).astype(BF16)

    vr_ref[...] = _dot(h, win_ref[:, OFF_V_R:OFF_G_R]).astype(BF16)
    g = _dot(h, win_ref[:, OFF_G_R:OFF_GATE])
    sg_ref[...] = (g * jax.nn.sigmoid(g)).astype(BF16)
    gate_ref[...] = jax.nn.sigmoid(_dot(h, win_ref[:, OFF_GATE:IN_WIDTH_PADDED])).astype(BF16)


def _proj_call(x2, pos2, gmix, win, gqa, wqb, gkva, wkvb, gqn, gknn, gknr, invm, invr):
    t = x2.shape[0]
    tm = TM_PROJ
    row = lambda w: pl.BlockSpec((tm, w), lambda i: (i, 0))
    out_shape = (
        jax.ShapeDtypeStruct((t, MLA_HEADS * LANES), BF16),
        jax.ShapeDtypeStruct((t, MLA_HEADS * LANES), BF16),
        jax.ShapeDtypeStruct((MLA_HEADS * MLA_V, t), BF16),
        jax.ShapeDtypeStruct((t, RET_HEADS * RET_QK), BF16),
        jax.ShapeDtypeStruct((t, RET_HEADS * RET_QK), BF16),
        jax.ShapeDtypeStruct((t, RET_HEADS * RET_V), BF16),
        jax.ShapeDtypeStruct((t, RET_HEADS * RET_V), BF16),
        jax.ShapeDtypeStruct((t, 2 * D_MODEL), BF16),
    )
    return pl.pallas_call(
        _proj_kernel,
        grid=(t // tm,),
        in_specs=[
            row(D_MODEL), row(1),
            _const_spec(gmix.shape), _const_spec(win.shape), _const_spec(gqa.shape),
            _const_spec(wqb.shape), _const_spec(gkva.shape), _const_spec(wkvb.shape),
            _const_spec(gqn.shape), _const_spec(gknn.shape), _const_spec(gknr.shape),
            _const_spec(invm.shape), _const_spec(invr.shape),
        ],
        out_specs=(
            row(MLA_HEADS * LANES), row(MLA_HEADS * LANES),
            pl.BlockSpec((MLA_HEADS * MLA_V, tm), lambda i: (0, i)),
            row(RET_HEADS * RET_QK), row(RET_HEADS * RET_QK),
            row(RET_HEADS * RET_V), row(RET_HEADS * RET_V), row(2 * D_MODEL),
        ),
        out_shape=out_shape,
        compiler_params=pltpu.CompilerParams(
            dimension_semantics=("arbitrary",), vmem_limit_bytes=VMEM_LIMIT_BYTES),
        name="proj_in",
    )(x2, pos2, gmix, win, gqa, wqb, gkva, wkvb, gqn, gknn, gknr, invm, invr)


def _attn_kernel(q_ref, k_ref, vt_ref, o_ref, *, seq):
    n_q = seq // TQ
    n_k = seq // TK

    def q_tile(qi, carry):
        q0 = pl.multiple_of(qi * TQ, TQ)
        qt = q_ref[pl.ds(q0, TQ), :].astype(F32).T.astype(BF16)

        def k_chunk(ki, state):
            m, l, acc = state
            k0 = pl.multiple_of(ki * TK, TK)
            s = _dot(k_ref[pl.ds(k0, TK), :], qt)
            m_new = jnp.maximum(m, jnp.max(s, axis=0, keepdims=True))
            p = jnp.exp(s - m_new)
            alpha = jnp.exp(m - m_new)
            l = alpha * l + jnp.sum(p, axis=0, keepdims=True)
            acc = alpha * acc + _dot(vt_ref[:, pl.ds(k0, TK)], p.astype(BF16))
            return m_new, l, acc

        init = (jnp.full((1, TQ), -jnp.inf, F32), jnp.zeros((1, TQ), F32),
                jnp.zeros((MLA_V, TQ), F32))
        _, l, acc = lax.fori_loop(0, n_k, k_chunk, init)
        o_ref[:, pl.ds(q0, TQ)] = (acc / l).astype(BF16)
        return carry

    lax.fori_loop(0, n_q, q_tile, 0)


def _attn_call(q, k, vt, batch, seq):
    t = batch * seq
    return pl.pallas_call(
        functools.partial(_attn_kernel, seq=seq),
        grid=(batch, MLA_HEADS),
        in_specs=[
            pl.BlockSpec((seq, LANES), lambda b, h: (b, h)),
            pl.BlockSpec((seq, LANES), lambda b, h: (b, h)),
            pl.BlockSpec((MLA_V, seq), lambda b, h: (h, b)),
        ],
        out_specs=pl.BlockSpec((MLA_V, seq), lambda b, h: (h, b)),
        out_shape=jax.ShapeDtypeStruct((MLA_HEADS * MLA_V, t), BF16),
        compiler_params=pltpu.CompilerParams(
            dimension_semantics=("arbitrary", "arbitrary"), vmem_limit_bytes=VMEM_LIMIT_BYTES),
        name="mla_attention",
    )(q, k, vt)


def _ret_kernel(dec_ref, q_ref, k_ref, v_ref, sg_ref, o_ref, sb_ref, *, seq):
    c = RET_C
    n = seq // c
    hd = pl.program_id(1)
    lane = lax.broadcasted_iota(jnp.int32, (1, LANES), 1)
    lo = (hd % 2) * RET_QK
    hmask = jnp.where((lane >= lo) & (lane < lo + RET_QK), 1.0, 0.0).astype(F32)

    lg_f = -jnp.exp(jnp.full((1, 1), dec_ref[0, hd], F32))
    lg_b = -jnp.exp(jnp.full((1, 1), dec_ref[1, hd], F32))
    a = lax.broadcasted_iota(jnp.int32, (c, 1), 0).astype(F32)
    b = lax.broadcasted_iota(jnp.int32, (1, c), 1).astype(F32)
    qd_f = jnp.exp(lg_f * (a + 1.0)) * hmask
    kd_f = jnp.exp(lg_f * (c - 1.0 - a)) * hmask
    qd_b = jnp.exp(lg_b * (c - a)) * hmask
    kd_b = jnp.exp(lg_b * a) * hmask
    cd_f = jnp.exp(lg_f * float(c))
    cd_b = jnp.exp(lg_b * float(c))
    diff = a - b
    decay = jnp.where(diff >= 0, jnp.exp(lg_f * jnp.maximum(diff, 0.0)),
                      jnp.exp(lg_b * jnp.maximum(-diff, 0.0)))

    def kv_state(k_f32, v_bf, kd):
        return _dot((k_f32 * kd).T.astype(BF16), v_bf)

    def bwd_pass(j, state):
        i = n - 1 - j
        r0 = pl.multiple_of(i * c, c)
        sb_ref[i] = state.astype(BF16)
        kv = kv_state(k_ref[pl.ds(r0, c), :].astype(F32), v_ref[pl.ds(r0, c), :], kd_b)
        return state * cd_b + kv

    lax.fori_loop(0, n, bwd_pass, jnp.zeros((LANES, RET_V), F32))

    def fwd_pass(i, state):
        r0 = pl.multiple_of(i * c, c)
        q = q_ref[pl.ds(r0, c), :].astype(F32)
        k_bf = k_ref[pl.ds(r0, c), :]
        v_bf = v_ref[pl.ds(r0, c), :]
        scores = lax.dot_general((q * hmask).astype(BF16), k_bf, (((1,), (1,)), ((), ())),
                                 preferred_element_type=F32)
        o = _dot((scores * decay).astype(BF16), v_bf)
        o = o + _dot((q * qd_f).astype(BF16), state.astype(BF16))
        o = o + _dot((q * qd_b).astype(BF16), sb_ref[i])
        o = o * _rms_scale(o, RET_V)
        o_ref[pl.ds(r0, c), :] = (sg_ref[pl.ds(r0, c), :].astype(F32) * o).astype(BF16)
        return state * cd_f + kv_state(k_bf.astype(F32), v_bf, kd_f)

    lax.fori_loop(0, n, fwd_pass, jnp.zeros((LANES, RET_V), F32))


def _ret_call(dec, qr, kr, vr, sg, batch, seq):
    t = batch * seq
    pair = pl.BlockSpec((seq, LANES), lambda b, h: (b, h // 2))
    head = pl.BlockSpec((seq, LANES), lambda b, h: (b, h))
    return pl.pallas_call(
        functools.partial(_ret_kernel, seq=seq),
        grid=(batch, RET_HEADS),
        in_specs=[pl.BlockSpec(memory_space=pltpu.SMEM), pair, pair, head, head],
        out_specs=head,
        out_shape=jax.ShapeDtypeStruct((t, RET_HEADS * RET_V), BF16),
        scratch_shapes=[pltpu.VMEM((seq // RET_C, LANES, RET_V), BF16)],
        compiler_params=pltpu.CompilerParams(
            dimension_semantics=("arbitrary", "arbitrary"), vmem_limit_bytes=VMEM_LIMIT_BYTES),
        name="retention",
    )(dec, qr, kr, vr, sg)


def _out_kernel(x_ref, ot_ref, ob_ref, gate_ref, wa_ref, wb_ref, wo_ref, gffn_ref, wgu_ref, wd_ref,
                y_ref, act_ref):
    o_a = ot_ref[...].astype(F32).T.astype(BF16)
    y_a = _dot(o_a, wa_ref[...])
    y_b = _dot(ob_ref[...], wb_ref[...])
    merged = (gate_ref[:, :D_MODEL].astype(F32) * y_a
              + gate_ref[:, D_MODEL:].astype(F32) * y_b).astype(BF16)
    x1 = x_ref[...] + _dot(merged, wo_ref[...])
    h2 = (x1 * _rms_scale(x1, D_MODEL) * gffn_ref[...]).astype(BF16)
    for cidx in range(FFN_HIDDEN // FFN_CHUNK):
        c0 = cidx * FFN_CHUNK
        gate = _dot(h2, wgu_ref[:, c0:c0 + FFN_CHUNK])
        up = _dot(h2, wgu_ref[:, FFN_HIDDEN + c0:FFN_HIDDEN + c0 + FFN_CHUNK])
        act_ref[:, c0:c0 + FFN_CHUNK] = (gate * jax.nn.sigmoid(gate) * up).astype(BF16)
    y_ref[...] = x1 + _dot(act_ref[...], wd_ref[...])


def _out_call(x2, ot, ob, gates, wa, wb, wo, gffn, wgu, wd):
    t = x2.shape[0]
    tm = TM_OUT
    row = lambda w: pl.BlockSpec((tm, w), lambda i: (i, 0))
    return pl.pallas_call(
        _out_kernel,
        grid=(t // tm,),
        in_specs=[
            row(D_MODEL),
            pl.BlockSpec((MLA_HEADS * MLA_V, tm), lambda i: (0, i)),
            row(RET_HEADS * RET_V), row(2 * D_MODEL),
            _const_spec(wa.shape), _const_spec(wb.shape), _const_spec(wo.shape),
            _const_spec(gffn.shape), _const_spec(wgu.shape), _const_spec(wd.shape),
        ],
        out_specs=row(D_MODEL),
        out_shape=jax.ShapeDtypeStruct((t, D_MODEL), F32),
        scratch_shapes=[pltpu.VMEM((tm, FFN_HIDDEN), BF16)],
        compiler_params=pltpu.CompilerParams(
            dimension_semantics=("arbitrary",), vmem_limit_bytes=VMEM_LIMIT_BYTES),
        name="merge_ffn",
    )(x2, ot, ob, gates, wa, wb, wo, gffn, wgu, wd)


def _pad_cols(w, left, right):
    return jnp.pad(w, ((0, 0), (left, right)))


def _layout_w_in(w_in):
    o = 0
    c_q = w_in[:, o:o + MLA_Q_RANK]; o += MLA_Q_RANK
    c_kv = w_in[:, o:o + MLA_KV_RANK]; o += MLA_KV_RANK
    k_rope = w_in[:, o:o + MLA_ROPE]; o += MLA_ROPE
    rest = w_in[:, o:]
    k_rope_blk = _pad_cols(k_rope, MLA_NOPE, LANES - MLA_NOPE - MLA_ROPE)
    return jnp.concatenate([c_q, c_kv, k_rope_blk, rest], axis=1).astype(BF16)


def _layout_w_q_b(w_q_b):
    w = w_q_b.reshape(MLA_Q_RANK, MLA_HEADS, MLA_QK)
    w = jnp.pad(w, ((0, 0), (0, 0), (0, LANES - MLA_QK)))
    return w.reshape(MLA_Q_RANK, MLA_HEADS * LANES).astype(BF16)


def _layout_w_kv_b(w_kv_b):
    w = w_kv_b.reshape(MLA_KV_RANK, MLA_HEADS, MLA_NOPE + MLA_V)
    k_nope = jnp.pad(w[:, :, :MLA_NOPE], ((0, 0), (0, 0), (0, LANES - MLA_NOPE)))
    v = w[:, :, MLA_NOPE:]
    return jnp.concatenate([k_nope.reshape(MLA_KV_RANK, MLA_HEADS * LANES),
                            v.reshape(MLA_KV_RANK, MLA_HEADS * MLA_V)], axis=1).astype(BF16)


def _rope_freqs():
    lane = np.arange(LANES)
    half_m = MLA_ROPE // 2
    inv_m = np.zeros((1, LANES), np.float64)
    rope_lanes = (lane >= MLA_NOPE) & (lane < MLA_QK)
    idx_m = (lane - MLA_NOPE) % half_m
    inv_m[0, rope_lanes] = ROPE_THETA ** (-idx_m[rope_lanes] / half_m)
    half_r = RET_QK // 2
    inv_r = (ROPE_THETA ** (-(lane % half_r) / half_r)).reshape(1, LANES)
    return jnp.asarray(inv_m, F32), jnp.asarray(inv_r, F32)


def kernel(x, positions, g_mix, w_in, g_q_a, w_q_b, g_kv_a, w_kv_b, g_qn, g_kn, w_mla_out,
           ret_decay_fwd, ret_decay_bwd, w_ret_out, w_out, g_ffn, w_gate_up, w_down):
    batch, seq, d = x.shape
    t = batch * seq
    x2 = x.reshape(t, d)
    pos2 = positions.reshape(t, 1).astype(jnp.int32)
    inv_m, inv_r = _rope_freqs()

    row = lambda v: v.astype(F32).reshape(1, -1)
    gqn = jnp.pad(row(g_qn), ((0, 0), (0, LANES - MLA_QK))) * (MLA_QK ** -0.5)
    gknn = jnp.pad(row(g_kn)[:, :MLA_NOPE], ((0, 0), (0, LANES - MLA_NOPE)))
    gknr = jnp.pad(row(g_kn)[:, MLA_NOPE:], ((0, 0), (MLA_NOPE, LANES - MLA_QK)))

    q, k, vt, qr, kr, vr, sg, gates = _proj_call(
        x2, pos2, row(g_mix), _layout_w_in(w_in), row(g_q_a), _layout_w_q_b(w_q_b),
        row(g_kv_a), _layout_w_kv_b(w_kv_b), gqn, gknn, gknr, inv_m, inv_r)

    ot = _attn_call(q, k, vt, batch, seq)
    dec = jnp.stack([ret_decay_fwd, ret_decay_bwd]).astype(F32)
    ob = _ret_call(dec, qr, kr, vr, sg, batch, seq)

    y = _out_call(x2, ot, ob, gates, w_mla_out.astype(BF16), w_ret_out.astype(BF16),
                  w_out.astype(BF16), row(g_ffn), w_gate_up.astype(BF16), w_down.astype(BF16))
    return y.reshape(batch, seq, d)
```

```python
import functools
import math

import numpy as np
import jax
import jax.numpy as jnp
from jax import lax
from jax.experimental import pallas as pl
from jax.experimental.pallas import tpu as pltpu

F32 = jnp.float32
BF16 = jnp.bfloat16

LANES = 128
VMEM_LIMIT_BYTES = 56 * 1024 * 1024

D_MODEL = 1024
MLA_HEADS = 8
MLA_Q_RANK = 256
MLA_KV_RANK = 128
MLA_NOPE = 64
MLA_ROPE = 32
MLA_V = 64
MLA_QK = MLA_NOPE + MLA_ROPE
RET_HEADS = 8
RET_QK = 64
RET_V = 128
FFN_HIDDEN = 2816
ROPE_THETA = 10000.0
EPS = 1e-6

OFF_SMALL = 0
OFF_QK_R = 512
OFF_V_R = OFF_QK_R + 2 * RET_HEADS * RET_QK
OFF_G_R = OFF_V_R + RET_HEADS * RET_V
OFF_GATE = OFF_G_R + RET_HEADS * RET_V
IN_WIDTH_PADDED = OFF_GATE + 2 * D_MODEL

TM_PROJ = 256
TM_OUT = 256
TQ = 256
TK = 512
S_AHEAD = 2
S_SLOTS = 4
BF16_SUBLANES = 16
VT_ROWS = MLA_V + BF16_SUBLANES
RET_C = 256
FFN_CHUNK = 256


def _dot(a, b):
    return jnp.dot(a, b, preferred_element_type=F32)


def _rms_scale(v, n):
    return lax.rsqrt(jnp.sum(v * v, axis=-1, keepdims=True) * (1.0 / n) + EPS)


def _const_spec(shape):
    zeros = (0,) * len(shape)
    return pl.BlockSpec(shape, lambda *_: zeros, pipeline_mode=pl.Buffered(1))


def _proj_kernel(x_ref, pos_ref, gmix_ref, win_ref, gqa_ref, wqb_ref, gkva_ref, wkvb_ref,
                 gqn_ref, gknn_ref, gknr_ref, invm_ref, invr_ref,
                 q_ref, k_ref, vt_ref, qr_ref, kr_ref, vr_ref, sg_ref, gate_ref):
    x = x_ref[...]
    h = (x * _rms_scale(x, D_MODEL) * gmix_ref[...]).astype(BF16)

    pos = pos_ref[...].astype(F32)
    lane = lax.broadcasted_iota(jnp.int32, (1, LANES), 1)
    ang = pos * invm_ref[...]
    cs_m = jnp.cos(ang)
    sn = jnp.sin(ang)
    lo_m = jnp.where(lane < MLA_NOPE + MLA_ROPE // 2, -sn, 0.0)
    hi_m = jnp.where(lane >= MLA_NOPE + MLA_ROPE // 2, sn, 0.0)
    ang = pos * invr_ref[...]
    cs_r = jnp.cos(ang)
    sn = jnp.sin(ang)
    first_half = (lane % RET_QK) < RET_QK // 2
    lo_r = jnp.where(first_half, -sn, 0.0)
    hi_r = jnp.where(first_half, 0.0, sn)

    def rope_m(v):
        return (v * cs_m + pltpu.roll(v, LANES - MLA_ROPE // 2, 1) * lo_m
                + pltpu.roll(v, MLA_ROPE // 2, 1) * hi_m)

    def rope_r(v):
        return (v * cs_r + pltpu.roll(v, LANES - RET_QK // 2, 1) * lo_r
                + pltpu.roll(v, RET_QK // 2, 1) * hi_r)

    small = _dot(h, win_ref[:, OFF_SMALL:OFF_QK_R])
    cq = small[:, :MLA_Q_RANK]
    ckv = small[:, MLA_Q_RANK:MLA_Q_RANK + MLA_KV_RANK]
    krb = small[:, MLA_Q_RANK + MLA_KV_RANK:]

    cqn = (cq * _rms_scale(cq, MLA_Q_RANK) * gqa_ref[...]).astype(BF16)
    qf = _dot(cqn, wqb_ref[...])
    gqn = gqn_ref[...]
    for hd in range(MLA_HEADS):
        blk = qf[:, hd * LANES:(hd + 1) * LANES]
        qn = blk * _rms_scale(blk, MLA_QK) * gqn
        q_ref[:, hd * LANES:(hd + 1) * LANES] = rope_m(qn).astype(BF16)

    ckvn = (ckv * _rms_scale(ckv, MLA_KV_RANK) * gkva_ref[...]).astype(BF16)
    kvf = _dot(ckvn, wkvb_ref[...])
    kr_roped = rope_m(krb * gknr_ref[...])
    ss_rope = jnp.sum(krb * krb, axis=-1, keepdims=True)
    gknn = gknn_ref[...]
    for hd in range(MLA_HEADS):
        blk = kvf[:, hd * LANES:(hd + 1) * LANES]
        ss = jnp.sum(blk * blk, axis=-1, keepdims=True) + ss_rope
        r = lax.rsqrt(ss * (1.0 / MLA_QK) + EPS)
        k_ref[:, hd * LANES:(hd + 1) * LANES] = ((blk * gknn + kr_roped) * r).astype(BF16)
    vt = kvf[:, MLA_HEADS * LANES:].T.astype(BF16)
    ones_rows = jnp.where(lax.broadcasted_iota(jnp.int32, (BF16_SUBLANES, vt.shape[1]), 0) == 0,
                          1.0, 0.0).astype(BF16)
    for hd in range(MLA_HEADS):
        vt_ref[hd * VT_ROWS:hd * VT_ROWS + MLA_V, :] = vt[hd * MLA_V:(hd + 1) * MLA_V]
        vt_ref[hd * VT_ROWS + MLA_V:(hd + 1) * VT_ROWS, :] = ones_rows

    qk_r = _dot(h, win_ref[:, OFF_QK_R:OFF_V_R])
    half = RET_HEADS * RET_QK
    for j in range(half // LANES):
        qr_ref[:, j * LANES:(j + 1) * LANES] = rope_r(qk_r[:, j * LANES:(j + 1) * LANES]).astype(BF16)
        kb = qk_r[:, half + j * LANES:half + (j + 1) * LANES]
        kr_ref[:, j * LANES:(j + 1) * LANES] = (rope_r(kb) * (RET_QK ** -0.5)).astype(BF16)

    vr_ref[...] = _dot(h, win_ref[:, OFF_V_R:OFF_G_R]).astype(BF16)
    g = _dot(h, win_ref[:, OFF_G_R:OFF_GATE])
    sg_ref[...] = (g * jax.nn.sigmoid(g)).astype(BF16)
    gate_ref[...] = jax.nn.sigmoid(_dot(h, win_ref[:, OFF_GATE:IN_WIDTH_PADDED])).astype(BF16)


def _proj_call(x2, pos2, gmix, win, gqa, wqb, gkva, wkvb, gqn, gknn, gknr, invm, invr):
    t = x2.shape[0]
    tm = TM_PROJ
    row = lambda w: pl.BlockSpec((tm, w), lambda i: (i, 0))
    out_shape = (
        jax.ShapeDtypeStruct((t, MLA_HEADS * LANES), BF16),
        jax.ShapeDtypeStruct((t, MLA_HEADS * LANES), BF16),
        jax.ShapeDtypeStruct((MLA_HEADS * VT_ROWS, t), BF16),
        jax.ShapeDtypeStruct((t, RET_HEADS * RET_QK), BF16),
        jax.ShapeDtypeStruct((t, RET_HEADS * RET_QK), BF16),
        jax.ShapeDtypeStruct((t, RET_HEADS * RET_V), BF16),
        jax.ShapeDtypeStruct((t, RET_HEADS * RET_V), BF16),
        jax.ShapeDtypeStruct((t, 2 * D_MODEL), BF16),
    )
    return pl.pallas_call(
        _proj_kernel,
        grid=(t // tm,),
        in_specs=[
            row(D_MODEL), row(1),
            _const_spec(gmix.shape), _const_spec(win.shape), _const_spec(gqa.shape),
            _const_spec(wqb.shape), _const_spec(gkva.shape), _const_spec(wkvb.shape),
            _const_spec(gqn.shape), _const_spec(gknn.shape), _const_spec(gknr.shape),
            _const_spec(invm.shape), _const_spec(invr.shape),
        ],
        out_specs=(
            row(MLA_HEADS * LANES), row(MLA_HEADS * LANES),
            pl.BlockSpec((MLA_HEADS * VT_ROWS, tm), lambda i: (0, i)),
            row(RET_HEADS * RET_QK), row(RET_HEADS * RET_QK),
            row(RET_HEADS * RET_V), row(RET_HEADS * RET_V), row(2 * D_MODEL),
        ),
        out_shape=out_shape,
        compiler_params=pltpu.CompilerParams(
            dimension_semantics=("arbitrary",), vmem_limit_bytes=VMEM_LIMIT_BYTES),
        name="proj_in",
    )(x2, pos2, gmix, win, gqa, wqb, gkva, wkvb, gqn, gknn, gknr, invm, invr)


def _attn_kernel(q_ref, k_ref, vt_ref, o_ref, s_ref, *, seq):
    n_q = seq // TQ
    n_k = seq // TK
    assert n_k % S_SLOTS == 0 and n_k > S_AHEAD

    def q_transposed(qi):
        q0 = pl.multiple_of(qi * TQ, TQ)
        return q_ref[pl.ds(q0, TQ), :].astype(F32).T.astype(BF16)

    def scores(qt, ki):
        s = _dot(k_ref[ki * TK:(ki + 1) * TK, :], qt)
        s_ref[ki % S_SLOTS] = s
        return jnp.max(s, axis=0, keepdims=True)

    def q_tile(qi, cms):
        qt = q_transposed(qi)
        qt_next = q_transposed(jnp.minimum(qi + 1, n_q - 1))
        m, cms = cms[0], cms[1:]
        acc = jnp.zeros((VT_ROWS, TQ), F32)
        for ki in range(n_k):
            ka = ki + S_AHEAD
            cms = cms + (scores(qt, ka) if ka < n_k else scores(qt_next, ka - n_k),)
            p = jnp.exp2(s_ref[ki % S_SLOTS] - m).astype(BF16)
            acc = acc + _dot(vt_ref[:, ki * TK:(ki + 1) * TK], p)
            if ki + 1 < n_k:
                m_new = jnp.maximum(m, cms[0])
                acc = acc * jnp.exp2(m - m_new)
                m, cms = m_new, cms[1:]
        q0 = pl.multiple_of(qi * TQ, TQ)
        o_ref[:, pl.ds(q0, TQ)] = (acc[:MLA_V] / acc[MLA_V:MLA_V + 1]).astype(BF16)
        return cms

    qt0 = q_transposed(0)
    lax.fori_loop(0, n_q, q_tile, tuple(scores(qt0, i) for i in range(S_AHEAD)))


def _attn_call(q, k, vt, batch, seq):
    t = batch * seq
    return pl.pallas_call(
        functools.partial(_attn_kernel, seq=seq),
        grid=(batch, MLA_HEADS),
        in_specs=[
            pl.BlockSpec((seq, LANES), lambda b, h: (b, h)),
            pl.BlockSpec((seq, LANES), lambda b, h: (b, h)),
            pl.BlockSpec((VT_ROWS, seq), lambda b, h: (h, b)),
        ],
        out_specs=pl.BlockSpec((MLA_V, seq), lambda b, h: (h, b)),
        out_shape=jax.ShapeDtypeStruct((MLA_HEADS * MLA_V, t), BF16),
        scratch_shapes=[pltpu.VMEM((S_SLOTS, TK, TQ), F32)],
        compiler_params=pltpu.CompilerParams(
            dimension_semantics=("arbitrary", "arbitrary"), vmem_limit_bytes=VMEM_LIMIT_BYTES),
        name="mla_attention",
    )(q, k, vt)


def _ret_kernel(dec_ref, q_ref, k_ref, v_ref, sg_ref, o_ref, sb_ref, *, seq):
    c = RET_C
    n = seq // c
    hd = pl.program_id(1)
    lane = lax.broadcasted_iota(jnp.int32, (1, LANES), 1)
    lo = (hd % 2) * RET_QK
    hmask = jnp.where((lane >= lo) & (lane < lo + RET_QK), 1.0, 0.0).astype(F32)

    lg_f = -jnp.exp(jnp.full((1, 1), dec_ref[0, hd], F32))
    lg_b = -jnp.exp(jnp.full((1, 1), dec_ref[1, hd], F32))
    a = lax.broadcasted_iota(jnp.int32, (c, 1), 0).astype(F32)
    b = lax.broadcasted_iota(jnp.int32, (1, c), 1).astype(F32)
    qd_f = jnp.exp(lg_f * (a + 1.0)) * hmask
    kd_f = jnp.exp(lg_f * (c - 1.0 - a)) * hmask
    qd_b = jnp.exp(lg_b * (c - a)) * hmask
    kd_b = jnp.exp(lg_b * a) * hmask
    cd_f = jnp.exp(lg_f * float(c))
    cd_b = jnp.exp(lg_b * float(c))
    diff = a - b
    decay = jnp.where(diff >= 0, jnp.exp(lg_f * jnp.maximum(diff, 0.0)),
                      jnp.exp(lg_b * jnp.maximum(-diff, 0.0)))

    def kv_state(k_f32, v_bf, kd):
        return _dot((k_f32 * kd).T.astype(BF16), v_bf)

    def bwd_pass(j, state):
        i = n - 1 - j
        r0 = pl.multiple_of(i * c, c)
        sb_ref[i] = state.astype(BF16)
        kv = kv_state(k_ref[pl.ds(r0, c), :].astype(F32), v_ref[pl.ds(r0, c), :], kd_b)
        return state * cd_b + kv

    lax.fori_loop(0, n, bwd_pass, jnp.zeros((LANES, RET_V), F32))

    def fwd_pass(i, state):
        r0 = pl.multiple_of(i * c, c)
        q = q_ref[pl.ds(r0, c), :].astype(F32)
        k_bf = k_ref[pl.ds(r0, c), :]
        v_bf = v_ref[pl.ds(r0, c), :]
        scores = lax.dot_general((q * hmask).astype(BF16), k_bf, (((1,), (1,)), ((), ())),
                                 preferred_element_type=F32)
        o = _dot((scores * decay).astype(BF16), v_bf)
        o = o + _dot((q * qd_f).astype(BF16), state.astype(BF16))
        o = o + _dot((q * qd_b).astype(BF16), sb_ref[i])
        o = o * _rms_scale(o, RET_V)
        o_ref[pl.ds(r0, c), :] = (sg_ref[pl.ds(r0, c), :].astype(F32) * o).astype(BF16)
        return state * cd_f + kv_state(k_bf.astype(F32), v_bf, kd_f)

    lax.fori_loop(0, n, fwd_pass, jnp.zeros((LANES, RET_V), F32))


def _ret_call(dec, qr, kr, vr, sg, batch, seq):
    t = batch * seq
    pair = pl.BlockSpec((seq, LANES), lambda b, h: (b, h // 2))
    head = pl.BlockSpec((seq, LANES), lambda b, h: (b, h))
    return pl.pallas_call(
        functools.partial(_ret_kernel, seq=seq),
        grid=(batch, RET_HEADS),
        in_specs=[pl.BlockSpec(memory_space=pltpu.SMEM), pair, pair, head, head],
        out_specs=head,
        out_shape=jax.ShapeDtypeStruct((t, RET_HEADS * RET_V), BF16),
        scratch_shapes=[pltpu.VMEM((seq // RET_C, LANES, RET_V), BF16)],
        compiler_params=pltpu.CompilerParams(
            dimension_semantics=("arbitrary", "arbitrary"), vmem_limit_bytes=VMEM_LIMIT_BYTES),
        name="retention",
    )(dec, qr, kr, vr, sg)


def _out_kernel(x_ref, ot_ref, ob_ref, gate_ref, wa_ref, wb_ref, wo_ref, gffn_ref, wgu_ref, wd_ref,
                y_ref, act_ref):
    o_a = ot_ref[...].astype(F32).T.astype(BF16)
    y_a = _dot(o_a, wa_ref[...])
    y_b = _dot(ob_ref[...], wb_ref[...])
    merged = (gate_ref[:, :D_MODEL].astype(F32) * y_a
              + gate_ref[:, D_MODEL:].astype(F32) * y_b).astype(BF16)
    x1 = x_ref[...] + _dot(merged, wo_ref[...])
    h2 = (x1 * _rms_scale(x1, D_MODEL) * gffn_ref[...]).astype(BF16)
    for cidx in range(FFN_HIDDEN // FFN_CHUNK):
        c0 = cidx * FFN_CHUNK
        gate = _dot(h2, wgu_ref[:, c0:c0 + FFN_CHUNK])
        up = _dot(h2, wgu_ref[:, FFN_HIDDEN + c0:FFN_HIDDEN + c0 + FFN_CHUNK])
        act_ref[:, c0:c0 + FFN_CHUNK] = (gate * jax.nn.sigmoid(gate) * up).astype(BF16)
    y_ref[...] = x1 + _dot(act_ref[...], wd_ref[...])


def _out_call(x2, ot, ob, gates, wa, wb, wo, gffn, wgu, wd):
    t = x2.shape[0]
    tm = TM_OUT
    row = lambda w: pl.BlockSpec((tm, w), lambda i: (i, 0))
    return pl.pallas_call(
        _out_kernel,
        grid=(t // tm,),
        in_specs=[
            row(D_MODEL),
            pl.BlockSpec((MLA_HEADS * MLA_V, tm), lambda i: (0, i)),
            row(RET_HEADS * RET_V), row(2 * D_MODEL),
            _const_spec(wa.shape), _const_spec(wb.shape), _const_spec(wo.shape),
            _const_spec(gffn.shape), _const_spec(wgu.shape), _const_spec(wd.shape),
        ],
        out_specs=row(D_MODEL),
        out_shape=jax.ShapeDtypeStruct((t, D_MODEL), F32),
        scratch_shapes=[pltpu.VMEM((tm, FFN_HIDDEN), BF16)],
        compiler_params=pltpu.CompilerParams(
            dimension_semantics=("arbitrary",), vmem_limit_bytes=VMEM_LIMIT_BYTES),
        name="merge_ffn",
    )(x2, ot, ob, gates, wa, wb, wo, gffn, wgu, wd)


def _pad_cols(w, left, right):
    return jnp.pad(w, ((0, 0), (left, right)))


def _layout_w_in(w_in):
    o = 0
    c_q = w_in[:, o:o + MLA_Q_RANK]; o += MLA_Q_RANK
    c_kv = w_in[:, o:o + MLA_KV_RANK]; o += MLA_KV_RANK
    k_rope = w_in[:, o:o + MLA_ROPE]; o += MLA_ROPE
    rest = w_in[:, o:]
    k_rope_blk = _pad_cols(k_rope, MLA_NOPE, LANES - MLA_NOPE - MLA_ROPE)
    return jnp.concatenate([c_q, c_kv, k_rope_blk, rest], axis=1).astype(BF16)


def _layout_w_q_b(w_q_b):
    w = w_q_b.reshape(MLA_Q_RANK, MLA_HEADS, MLA_QK)
    w = jnp.pad(w, ((0, 0), (0, 0), (0, LANES - MLA_QK)))
    return w.reshape(MLA_Q_RANK, MLA_HEADS * LANES).astype(BF16)


def _layout_w_kv_b(w_kv_b):
    w = w_kv_b.reshape(MLA_KV_RANK, MLA_HEADS, MLA_NOPE + MLA_V)
    k_nope = jnp.pad(w[:, :, :MLA_NOPE], ((0, 0), (0, 0), (0, LANES - MLA_NOPE)))
    v = w[:, :, MLA_NOPE:]
    return jnp.concatenate([k_nope.reshape(MLA_KV_RANK, MLA_HEADS * LANES),
                            v.reshape(MLA_KV_RANK, MLA_HEADS * MLA_V)], axis=1).astype(BF16)


def _rope_freqs():
    lane = np.arange(LANES)
    half_m = MLA_ROPE // 2
    inv_m = np.zeros((1, LANES), np.float64)
    rope_lanes = (lane >= MLA_NOPE) & (lane < MLA_QK)
    idx_m = (lane - MLA_NOPE) % half_m
    inv_m[0, rope_lanes] = ROPE_THETA ** (-idx_m[rope_lanes] / half_m)
    half_r = RET_QK // 2
    inv_r = (ROPE_THETA ** (-(lane % half_r) / half_r)).reshape(1, LANES)
    return jnp.asarray(inv_m, F32), jnp.asarray(inv_r, F32)


def kernel(x, positions, g_mix, w_in, g_q_a, w_q_b, g_kv_a, w_kv_b, g_qn, g_kn, w_mla_out,
           ret_decay_fwd, ret_decay_bwd, w_ret_out, w_out, g_ffn, w_gate_up, w_down):
    batch, seq, d = x.shape
    t = batch * seq
    x2 = x.reshape(t, d)
    pos2 = positions.reshape(t, 1).astype(jnp.int32)
    inv_m, inv_r = _rope_freqs()

    row = lambda v: v.astype(F32).reshape(1, -1)
    gqn = jnp.pad(row(g_qn), ((0, 0), (0, LANES - MLA_QK))) * (MLA_QK ** -0.5 * math.log2(math.e))
    gknn = jnp.pad(row(g_kn)[:, :MLA_NOPE], ((0, 0), (0, LANES - MLA_NOPE)))
    gknr = jnp.pad(row(g_kn)[:, MLA_NOPE:], ((0, 0), (MLA_NOPE, LANES - MLA_QK)))

    q, k, vt, qr, kr, vr, sg, gates = _proj_call(
        x2, pos2, row(g_mix), _layout_w_in(w_in), row(g_q_a), _layout_w_q_b(w_q_b),
        row(g_kv_a), _layout_w_kv_b(w_kv_b), gqn, gknn, gknr, inv_m, inv_r)

    ot = _attn_call(q, k, vt, batch, seq)
    dec = jnp.stack([ret_decay_fwd, ret_decay_bwd]).astype(F32)
    ob = _ret_call(dec, qr, kr, vr, sg, batch, seq)

    y = _out_call(x2, ot, ob, gates, w_mla_out.astype(BF16), w_ret_out.astype(BF16),
                  w_out.astype(BF16), row(g_ffn), w_gate_up.astype(BF16), w_down.astype(BF16))
    return y.reshape(batch, seq, d)
```

```python
import functools
import math

import numpy as np
import jax
import jax.numpy as jnp
from jax import lax
from jax.experimental import pallas as pl
from jax.experimental.pallas import tpu as pltpu

F32 = jnp.float32
BF16 = jnp.bfloat16

LANES = 128
VMEM_LIMIT_BYTES = 56 * 1024 * 1024

D_MODEL = 1024
MLA_HEADS = 8
MLA_Q_RANK = 256
MLA_KV_RANK = 128
MLA_NOPE = 64
MLA_ROPE = 32
MLA_V = 64
MLA_QK = MLA_NOPE + MLA_ROPE
RET_HEADS = 8
RET_QK = 64
RET_V = 128
FFN_HIDDEN = 2816
ROPE_THETA = 10000.0
EPS = 1e-6

OFF_SMALL = 0
OFF_QK_R = 512
OFF_V_R = OFF_QK_R + 2 * RET_HEADS * RET_QK
OFF_G_R = OFF_V_R + RET_HEADS * RET_V
OFF_GATE = OFF_G_R + RET_HEADS * RET_V
IN_WIDTH_PADDED = OFF_GATE + 2 * D_MODEL

TM_PROJ = 256
TM_OUT = 256
TQ = 512
TK = 256
S_AHEAD = 2
S_SLOTS = 4
BF16_SUBLANES = 16
VT_ROWS = MLA_V + BF16_SUBLANES
RET_C = 256
RET_AHEAD = 2
RET_SLOTS = 4
FFN_CHUNK = 256


def _dot(a, b):
    return jnp.dot(a, b, preferred_element_type=F32)


def _rms_scale(v, n):
    return lax.rsqrt(jnp.sum(v * v, axis=-1, keepdims=True) * (1.0 / n) + EPS)


def _const_spec(shape):
    zeros = (0,) * len(shape)
    return pl.BlockSpec(shape, lambda *_: zeros, pipeline_mode=pl.Buffered(1))


def _proj_kernel(x_ref, pos_ref, gmix_ref, win_ref, gqa_ref, wqb_ref, gkva_ref, wkvb_ref,
                 gqn_ref, gknn_ref, gknr_ref, invm_ref, invr_ref,
                 q_ref, k_ref, vt_ref, qr_ref, kr_ref, vr_ref, sg_ref, gate_ref):
    x = x_ref[...]
    h = (x * _rms_scale(x, D_MODEL) * gmix_ref[...]).astype(BF16)

    pos = pos_ref[...].astype(F32)
    lane = lax.broadcasted_iota(jnp.int32, (1, LANES), 1)
    ang = pos * invm_ref[...]
    cs_m = jnp.cos(ang)
    sn = jnp.sin(ang)
    lo_m = jnp.where(lane < MLA_NOPE + MLA_ROPE // 2, -sn, 0.0)
    hi_m = jnp.where(lane >= MLA_NOPE + MLA_ROPE // 2, sn, 0.0)
    ang = pos * invr_ref[...]
    cs_r = jnp.cos(ang)
    sn = jnp.sin(ang)
    first_half = (lane % RET_QK) < RET_QK // 2
    lo_r = jnp.where(first_half, -sn, 0.0)
    hi_r = jnp.where(first_half, 0.0, sn)

    def rope_m(v):
        return (v * cs_m + pltpu.roll(v, LANES - MLA_ROPE // 2, 1) * lo_m
                + pltpu.roll(v, MLA_ROPE // 2, 1) * hi_m)

    def rope_r(v):
        return (v * cs_r + pltpu.roll(v, LANES - RET_QK // 2, 1) * lo_r
                + pltpu.roll(v, RET_QK // 2, 1) * hi_r)

    small = _dot(h, win_ref[:, OFF_SMALL:OFF_QK_R])
    cq = small[:, :MLA_Q_RANK]
    ckv = small[:, MLA_Q_RANK:MLA_Q_RANK + MLA_KV_RANK]
    krb = small[:, MLA_Q_RANK + MLA_KV_RANK:]

    cqn = (cq * _rms_scale(cq, MLA_Q_RANK) * gqa_ref[...]).astype(BF16)
    qf = _dot(cqn, wqb_ref[...])
    gqn = gqn_ref[...]
    for hd in range(MLA_HEADS):
        blk = qf[:, hd * LANES:(hd + 1) * LANES]
        qn = blk * _rms_scale(blk, MLA_QK) * gqn
        q_ref[:, hd * LANES:(hd + 1) * LANES] = rope_m(qn).astype(BF16)

    ckvn = (ckv * _rms_scale(ckv, MLA_KV_RANK) * gkva_ref[...]).astype(BF16)
    kvf = _dot(ckvn, wkvb_ref[...])
    kr_roped = rope_m(krb * gknr_ref[...])
    ss_rope = jnp.sum(krb * krb, axis=-1, keepdims=True)
    gknn = gknn_ref[...]
    for hd in range(MLA_HEADS):
        blk = kvf[:, hd * LANES:(hd + 1) * LANES]
        ss = jnp.sum(blk * blk, axis=-1, keepdims=True) + ss_rope
        r = lax.rsqrt(ss * (1.0 / MLA_QK) + EPS)
        k_ref[:, hd * LANES:(hd + 1) * LANES] = ((blk * gknn + kr_roped) * r).astype(BF16)
    vt = kvf[:, MLA_HEADS * LANES:].T.astype(BF16)
    ones_rows = jnp.where(lax.broadcasted_iota(jnp.int32, (BF16_SUBLANES, vt.shape[1]), 0) == 0,
                          1.0, 0.0).astype(BF16)
    for hd in range(MLA_HEADS):
        vt_ref[hd * VT_ROWS:hd * VT_ROWS + MLA_V, :] = vt[hd * MLA_V:(hd + 1) * MLA_V]
        vt_ref[hd * VT_ROWS + MLA_V:(hd + 1) * VT_ROWS, :] = ones_rows

    qk_r = _dot(h, win_ref[:, OFF_QK_R:OFF_V_R])
    half = RET_HEADS * RET_QK
    for j in range(half // LANES):
        qr_ref[:, j * LANES:(j + 1) * LANES] = rope_r(qk_r[:, j * LANES:(j + 1) * LANES]).astype(BF16)
        kb = qk_r[:, half + j * LANES:half + (j + 1) * LANES]
        kr_ref[:, j * LANES:(j + 1) * LANES] = (rope_r(kb) * (RET_QK ** -0.5)).astype(BF16)

    vr_ref[...] = _dot(h, win_ref[:, OFF_V_R:OFF_G_R]).astype(BF16)
    g = _dot(h, win_ref[:, OFF_G_R:OFF_GATE])
    sg_ref[...] = (g * jax.nn.sigmoid(g)).astype(BF16)
    gate_ref[...] = jax.nn.sigmoid(_dot(h, win_ref[:, OFF_GATE:IN_WIDTH_PADDED])).astype(BF16)


def _proj_call(x2, pos2, gmix, win, gqa, wqb, gkva, wkvb, gqn, gknn, gknr, invm, invr):
    t = x2.shape[0]
    tm = TM_PROJ
    row = lambda w: pl.BlockSpec((tm, w), lambda i: (i, 0))
    out_shape = (
        jax.ShapeDtypeStruct((t, MLA_HEADS * LANES), BF16),
        jax.ShapeDtypeStruct((t, MLA_HEADS * LANES), BF16),
        jax.ShapeDtypeStruct((MLA_HEADS * VT_ROWS, t), BF16),
        jax.ShapeDtypeStruct((t, RET_HEADS * RET_QK), BF16),
        jax.ShapeDtypeStruct((t, RET_HEADS * RET_QK), BF16),
        jax.ShapeDtypeStruct((t, RET_HEADS * RET_V), BF16),
        jax.ShapeDtypeStruct((t, RET_HEADS * RET_V), BF16),
        jax.ShapeDtypeStruct((t, 2 * D_MODEL), BF16),
    )
    return pl.pallas_call(
        _proj_kernel,
        grid=(t // tm,),
        in_specs=[
            row(D_MODEL), row(1),
            _const_spec(gmix.shape), _const_spec(win.shape), _const_spec(gqa.shape),
            _const_spec(wqb.shape), _const_spec(gkva.shape), _const_spec(wkvb.shape),
            _const_spec(gqn.shape), _const_spec(gknn.shape), _const_spec(gknr.shape),
            _const_spec(invm.shape), _const_spec(invr.shape),
        ],
        out_specs=(
            row(MLA_HEADS * LANES), row(MLA_HEADS * LANES),
            pl.BlockSpec((MLA_HEADS * VT_ROWS, tm), lambda i: (0, i)),
            row(RET_HEADS * RET_QK), row(RET_HEADS * RET_QK),
            row(RET_HEADS * RET_V), row(RET_HEADS * RET_V), row(2 * D_MODEL),
        ),
        out_shape=out_shape,
        compiler_params=pltpu.CompilerParams(
            dimension_semantics=("arbitrary",), vmem_limit_bytes=VMEM_LIMIT_BYTES),
        name="proj_in",
    )(x2, pos2, gmix, win, gqa, wqb, gkva, wkvb, gqn, gknn, gknr, invm, invr)


def _attn_kernel(q_ref, k_ref, vt_ref, o_ref, s_ref, *, seq):
    n_q = seq // TQ
    n_k = seq // TK
    assert n_k % S_SLOTS == 0 and n_k > S_AHEAD

    def q_transposed(qi):
        q0 = pl.multiple_of(qi * TQ, TQ)
        return q_ref[pl.ds(q0, TQ), :].astype(F32).T.astype(BF16)

    def scores(qt, ki):
        s = _dot(k_ref[ki * TK:(ki + 1) * TK, :], qt)
        s_ref[ki % S_SLOTS] = s
        return jnp.max(s, axis=0, keepdims=True)

    def q_tile(qi, cms):
        qt = q_transposed(qi)
        qt_next = q_transposed(jnp.minimum(qi + 1, n_q - 1))
        m, cms = cms[0], cms[1:]
        acc = jnp.zeros((VT_ROWS, TQ), F32)
        for ki in range(n_k):
            ka = ki + S_AHEAD
            cms = cms + (scores(qt, ka) if ka < n_k else scores(qt_next, ka - n_k),)
            p = jnp.exp2(s_ref[ki % S_SLOTS] - m).astype(BF16)
            acc = acc + _dot(vt_ref[:, ki * TK:(ki + 1) * TK], p)
            if ki + 1 < n_k:
                m_new = jnp.maximum(m, cms[0])
                acc = acc * jnp.exp2(m - m_new)
                m, cms = m_new, cms[1:]
        q0 = pl.multiple_of(qi * TQ, TQ)
        o_ref[:, pl.ds(q0, TQ)] = (acc[:MLA_V] / acc[MLA_V:MLA_V + 1]).astype(BF16)
        return cms

    qt0 = q_transposed(0)
    lax.fori_loop(0, n_q, q_tile, tuple(scores(qt0, i) for i in range(S_AHEAD)))


def _attn_call(q, k, vt, batch, seq):
    t = batch * seq
    return pl.pallas_call(
        functools.partial(_attn_kernel, seq=seq),
        grid=(batch, MLA_HEADS),
        in_specs=[
            pl.BlockSpec((seq, LANES), lambda b, h: (b, h)),
            pl.BlockSpec((seq, LANES), lambda b, h: (b, h)),
            pl.BlockSpec((VT_ROWS, seq), lambda b, h: (h, b)),
        ],
        out_specs=pl.BlockSpec((MLA_V, seq), lambda b, h: (h, b)),
        out_shape=jax.ShapeDtypeStruct((MLA_HEADS * MLA_V, t), BF16),
        scratch_shapes=[pltpu.VMEM((S_SLOTS, TK, TQ), F32)],
        compiler_params=pltpu.CompilerParams(
            dimension_semantics=("arbitrary", "arbitrary"), vmem_limit_bytes=VMEM_LIMIT_BYTES),
        name="mla_attention",
    )(q, k, vt)


def _ret_kernel(dec_ref, q_ref, k_ref, v_ref, sg_ref, o_ref, kv_ref, st_ref, a_ref, *, seq):
    c = RET_C
    n = seq // c
    hd = pl.program_id(1)
    lo = (hd % 2) * RET_QK
    lane = lax.broadcasted_iota(jnp.int32, (1, LANES), 1)
    own = (lane >= lo) & (lane < lo + RET_QK)
    row = lax.broadcasted_iota(jnp.int32, (LANES, 1), 0)
    own_row = (row >= lo) & (row < lo + RET_QK)
    own_bf = jnp.where(own, 1.0, 0.0).astype(BF16)

    lg_f = -jnp.exp(jnp.full((1, 1), dec_ref[0, hd], F32))
    lg_b = -jnp.exp(jnp.full((1, 1), dec_ref[1, hd], F32))
    a = lax.broadcasted_iota(jnp.int32, (c, 1), 0).astype(F32)
    b = lax.broadcasted_iota(jnp.int32, (1, c), 1).astype(F32)
    q_decay = jnp.where(own, jnp.exp(lg_f * (a + 1.0)), jnp.exp(lg_b * (c - a)))
    k_decay = jnp.where(own, jnp.exp(lg_f * (c - 1.0 - a)), jnp.exp(lg_b * a))
    cd_f = jnp.exp(lg_f * float(c))
    cd_b = jnp.exp(lg_b * float(c))
    diff = a - b
    decay = jnp.where(diff >= 0, jnp.exp(lg_f * jnp.maximum(diff, 0.0)),
                      jnp.exp(lg_b * jnp.maximum(-diff, 0.0)))

    def both_halves(v):
        return jnp.where(own, v, pltpu.roll(v, RET_QK, 1))

    def chunk_kv(i, carry):
        r0 = pl.multiple_of(i * c, c)
        k = k_ref[pl.ds(r0, c), :].astype(F32)
        kv_ref[i] = _dot((both_halves(k) * k_decay).T.astype(BF16), v_ref[pl.ds(r0, c), :])
        return carry

    lax.fori_loop(0, n, chunk_kv, 0, unroll=4)

    def fwd_scan(i, f):
        st_ref[i] = f.astype(BF16)
        return f * cd_f + kv_ref[i]

    lax.fori_loop(0, n, fwd_scan, jnp.zeros((LANES, RET_V), F32))

    def bwd_scan(j, s):
        i = n - 1 - j
        st_ref[i] = jnp.where(own_row, st_ref[i].astype(F32), s).astype(BF16)
        return s * cd_b + kv_ref[i]

    lax.fori_loop(0, n, bwd_scan, jnp.zeros((LANES, RET_V), F32))

    def scores(i):
        rows = slice(i * c, (i + 1) * c)
        a_ref[i % RET_SLOTS] = lax.dot_general(
            q_ref[rows, :] * own_bf, k_ref[rows, :], (((1,), (1,)), ((), ())),
            preferred_element_type=F32)

    for i in range(min(RET_AHEAD, n)):
        scores(i)
    for i in range(n):
        if i + RET_AHEAD < n:
            scores(i + RET_AHEAD)
        rows = slice(i * c, (i + 1) * c)
        p = (a_ref[i % RET_SLOTS] * decay).astype(BF16)
        qd = (both_halves(q_ref[rows, :].astype(F32)) * q_decay).astype(BF16)
        o = _dot(p, v_ref[rows, :]) + _dot(qd, st_ref[i])
        o = o * _rms_scale(o, RET_V)
        o_ref[rows, :] = (sg_ref[rows, :].astype(F32) * o).astype(BF16)


def _ret_call(dec, qr, kr, vr, sg, batch, seq):
    t = batch * seq
    pair = pl.BlockSpec((seq, LANES), lambda b, h: (b, h // 2))
    head = pl.BlockSpec((seq, LANES), lambda b, h: (b, h))
    return pl.pallas_call(
        functools.partial(_ret_kernel, seq=seq),
        grid=(batch, RET_HEADS),
        in_specs=[pl.BlockSpec(memory_space=pltpu.SMEM), pair, pair, head, head],
        out_specs=head,
        out_shape=jax.ShapeDtypeStruct((t, RET_HEADS * RET_V), BF16),
        scratch_shapes=[pltpu.VMEM((seq // RET_C, LANES, RET_V), F32),
                        pltpu.VMEM((seq // RET_C, LANES, RET_V), BF16),
                        pltpu.VMEM((RET_SLOTS, RET_C, RET_C), F32)],
        compiler_params=pltpu.CompilerParams(
            dimension_semantics=("arbitrary", "arbitrary"), vmem_limit_bytes=VMEM_LIMIT_BYTES),
        name="retention",
    )(dec, qr, kr, vr, sg)


def _out_kernel(x_ref, ot_ref, ob_ref, gate_ref, wa_ref, wb_ref, wo_ref, gffn_ref, wgu_ref, wd_ref,
                y_ref, act_ref):
    o_a = ot_ref[...].astype(F32).T.astype(BF16)
    y_a = _dot(o_a, wa_ref[...])
    y_b = _dot(ob_ref[...], wb_ref[...])
    merged = (gate_ref[:, :D_MODEL].astype(F32) * y_a
              + gate_ref[:, D_MODEL:].astype(F32) * y_b).astype(BF16)
    x1 = x_ref[...] + _dot(merged, wo_ref[...])
    h2 = (x1 * _rms_scale(x1, D_MODEL) * gffn_ref[...]).astype(BF16)
    for cidx in range(FFN_HIDDEN // FFN_CHUNK):
        c0 = cidx * FFN_CHUNK
        gate = _dot(h2, wgu_ref[:, c0:c0 + FFN_CHUNK])
        up = _dot(h2, wgu_ref[:, FFN_HIDDEN + c0:FFN_HIDDEN + c0 + FFN_CHUNK])
        act_ref[:, c0:c0 + FFN_CHUNK] = (gate * jax.nn.sigmoid(gate) * up).astype(BF16)
    y_ref[...] = x1 + _dot(act_ref[...], wd_ref[...])


def _out_call(x2, ot, ob, gates, wa, wb, wo, gffn, wgu, wd):
    t = x2.shape[0]
    tm = TM_OUT
    row = lambda w: pl.BlockSpec((tm, w), lambda i: (i, 0))
    return pl.pallas_call(
        _out_kernel,
        grid=(t // tm,),
        in_specs=[
            row(D_MODEL),
            pl.BlockSpec((MLA_HEADS * MLA_V, tm), lambda i: (0, i)),
            row(RET_HEADS * RET_V), row(2 * D_MODEL),
            _const_spec(wa.shape), _const_spec(wb.shape), _const_spec(wo.shape),
            _const_spec(gffn.shape), _const_spec(wgu.shape), _const_spec(wd.shape),
        ],
        out_specs=row(D_MODEL),
        out_shape=jax.ShapeDtypeStruct((t, D_MODEL), F32),
        scratch_shapes=[pltpu.VMEM((tm, FFN_HIDDEN), BF16)],
        compiler_params=pltpu.CompilerParams(
            dimension_semantics=("arbitrary",), vmem_limit_bytes=VMEM_LIMIT_BYTES),
        name="merge_ffn",
    )(x2, ot, ob, gates, wa, wb, wo, gffn, wgu, wd)


def _pad_cols(w, left, right):
    return jnp.pad(w, ((0, 0), (left, right)))


def _layout_w_in(w_in):
    o = 0
    c_q = w_in[:, o:o + MLA_Q_RANK]; o += MLA_Q_RANK
    c_kv = w_in[:, o:o + MLA_KV_RANK]; o += MLA_KV_RANK
    k_rope = w_in[:, o:o + MLA_ROPE]; o += MLA_ROPE
    rest = w_in[:, o:]
    k_rope_blk = _pad_cols(k_rope, MLA_NOPE, LANES - MLA_NOPE - MLA_ROPE)
    return jnp.concatenate([c_q, c_kv, k_rope_blk, rest], axis=1).astype(BF16)


def _layout_w_q_b(w_q_b):
    w = w_q_b.reshape(MLA_Q_RANK, MLA_HEADS, MLA_QK)
    w = jnp.pad(w, ((0, 0), (0, 0), (0, LANES - MLA_QK)))
    return w.reshape(MLA_Q_RANK, MLA_HEADS * LANES).astype(BF16)


def _layout_w_kv_b(w_kv_b):
    w = w_kv_b.reshape(MLA_KV_RANK, MLA_HEADS, MLA_NOPE + MLA_V)
    k_nope = jnp.pad(w[:, :, :MLA_NOPE], ((0, 0), (0, 0), (0, LANES - MLA_NOPE)))
    v = w[:, :, MLA_NOPE:]
    return jnp.concatenate([k_nope.reshape(MLA_KV_RANK, MLA_HEADS * LANES),
                            v.reshape(MLA_KV_RANK, MLA_HEADS * MLA_V)], axis=1).astype(BF16)


def _rope_freqs():
    lane = np.arange(LANES)
    half_m = MLA_ROPE // 2
    inv_m = np.zeros((1, LANES), np.float64)
    rope_lanes = (lane >= MLA_NOPE) & (lane < MLA_QK)
    idx_m = (lane - MLA_NOPE) % half_m
    inv_m[0, rope_lanes] = ROPE_THETA ** (-idx_m[rope_lanes] / half_m)
    half_r = RET_QK // 2
    inv_r = (ROPE_THETA ** (-(lane % half_r) / half_r)).reshape(1, LANES)
    return jnp.asarray(inv_m, F32), jnp.asarray(inv_r, F32)


def kernel(x, positions, g_mix, w_in, g_q_a, w_q_b, g_kv_a, w_kv_b, g_qn, g_kn, w_mla_out,
           ret_decay_fwd, ret_decay_bwd, w_ret_out, w_out, g_ffn, w_gate_up, w_down):
    batch, seq, d = x.shape
    t = batch * seq
    x2 = x.reshape(t, d)
    pos2 = positions.reshape(t, 1).astype(jnp.int32)
    inv_m, inv_r = _rope_freqs()

    row = lambda v: v.astype(F32).reshape(1, -1)
    gqn = jnp.pad(row(g_qn), ((0, 0), (0, LANES - MLA_QK))) * (MLA_QK ** -0.5 * math.log2(math.e))
    gknn = jnp.pad(row(g_kn)[:, :MLA_NOPE], ((0, 0), (0, LANES - MLA_NOPE)))
    gknr = jnp.pad(row(g_kn)[:, MLA_NOPE:], ((0, 0), (MLA_NOPE, LANES - MLA_QK)))

    q, k, vt, qr, kr, vr, sg, gates = _proj_call(
        x2, pos2, row(g_mix), _layout_w_in(w_in), row(g_q_a), _layout_w_q_b(w_q_b),
        row(g_kv_a), _layout_w_kv_b(w_kv_b), gqn, gknn, gknr, inv_m, inv_r)

    ot = _attn_call(q, k, vt, batch, seq)
    dec = jnp.stack([ret_decay_fwd, ret_decay_bwd]).astype(F32)
    ob = _ret_call(dec, qr, kr, vr, sg, batch, seq)

    y = _out_call(x2, ot, ob, gates, w_mla_out.astype(BF16), w_ret_out.astype(BF16),
                  w_out.astype(BF16), row(g_ffn), w_gate_up.astype(BF16), w_down.astype(BF16))
    return y.reshape(batch, seq, d)
```

```python
import functools
import math

import numpy as np
import jax
import jax.numpy as jnp
from jax import lax
from jax.experimental import pallas as pl
from jax.experimental.pallas import tpu as pltpu

F32 = jnp.float32
BF16 = jnp.bfloat16

LANES = 128
VMEM_LIMIT_BYTES = 56 * 1024 * 1024

D_MODEL = 1024
MLA_HEADS = 8
MLA_Q_RANK = 256
MLA_KV_RANK = 128
MLA_NOPE = 64
MLA_ROPE = 32
MLA_V = 64
MLA_QK = MLA_NOPE + MLA_ROPE
RET_HEADS = 8
RET_QK = 64
RET_V = 128
FFN_HIDDEN = 2816
ROPE_THETA = 10000.0
EPS = 1e-6

OFF_QK_R = 0
OFF_V_R = OFF_QK_R + 2 * RET_HEADS * RET_QK
OFF_G_R = OFF_V_R + RET_HEADS * RET_V
OFF_GATE = OFF_G_R + RET_HEADS * RET_V
OFF_END = OFF_GATE + 2 * D_MODEL

TM_PROJ = 512
TM_OUT = 256
TQ = 512
TK = 256
S_AHEAD = 2
S_SLOTS = 4
BF16_SUBLANES = 16
VT_ROWS = MLA_V + BF16_SUBLANES
RET_C = 256
RET_AHEAD = 2
RET_SLOTS = 4
FFN_CHUNK = 256


def _dot(a, b):
    return jnp.dot(a, b, preferred_element_type=F32)


def _rms_scale(v, n):
    return lax.rsqrt(jnp.sum(v * v, axis=-1, keepdims=True) * (1.0 / n) + EPS)


def _const_spec(shape):
    zeros = (0,) * len(shape)
    return pl.BlockSpec(shape, lambda *_: zeros, pipeline_mode=pl.Buffered(1))


def _proj_kernel(x_ref, pos_ref, gmix_ref, wsm_ref, win_ref, gqa_ref, wqb_ref, gkva_ref, wkvb_ref,
                 gqn_ref, gknn_ref, gknr_ref, invm_ref, invr_ref,
                 q_ref, k_ref, vt_ref, qr_ref, kr_ref, vr_ref, sg_ref, gate_ref):
    x = x_ref[...]
    h = (x * _rms_scale(x, D_MODEL) * gmix_ref[...]).astype(BF16)

    pos = pos_ref[...].astype(F32)
    lane = lax.broadcasted_iota(jnp.int32, (1, LANES), 1)
    ang = pos * invm_ref[...]
    cs_m = jnp.cos(ang)
    sn = jnp.sin(ang)
    lo_m = jnp.where(lane < MLA_NOPE + MLA_ROPE // 2, -sn, 0.0)
    hi_m = jnp.where(lane >= MLA_NOPE + MLA_ROPE // 2, sn, 0.0)
    ang = pos * invr_ref[...]
    cs_r = jnp.cos(ang)
    sn = jnp.sin(ang)
    first_half = (lane % RET_QK) < RET_QK // 2
    lo_r = jnp.where(first_half, -sn, 0.0)
    hi_r = jnp.where(first_half, 0.0, sn)

    def rope_m(v):
        return (v * cs_m + pltpu.roll(v, LANES - MLA_ROPE // 2, 1) * lo_m
                + pltpu.roll(v, MLA_ROPE // 2, 1) * hi_m)

    def rope_r(v):
        return (v * cs_r + pltpu.roll(v, LANES - RET_QK // 2, 1) * lo_r
                + pltpu.roll(v, RET_QK // 2, 1) * hi_r)

    small = _dot(h, wsm_ref[...])
    cq = small[:, :MLA_Q_RANK]
    ckv = small[:, MLA_Q_RANK:MLA_Q_RANK + MLA_KV_RANK]
    krb = small[:, MLA_Q_RANK + MLA_KV_RANK:]

    cqn = (cq * _rms_scale(cq, MLA_Q_RANK) * gqa_ref[...]).astype(BF16)
    qf = _dot(cqn, wqb_ref[...])
    gqn = gqn_ref[...]
    for hd in range(MLA_HEADS):
        blk = qf[:, hd * LANES:(hd + 1) * LANES]
        qn = blk * _rms_scale(blk, MLA_QK) * gqn
        q_ref[:, hd * LANES:(hd + 1) * LANES] = rope_m(qn).astype(BF16)

    ckvn = (ckv * _rms_scale(ckv, MLA_KV_RANK) * gkva_ref[...]).astype(BF16)
    kvf = _dot(ckvn, wkvb_ref[...])
    kr_roped = rope_m(krb * gknr_ref[...])
    ss_rope = jnp.sum(krb * krb, axis=-1, keepdims=True)
    gknn = gknn_ref[...]
    for hd in range(MLA_HEADS):
        blk = kvf[:, hd * LANES:(hd + 1) * LANES]
        ss = jnp.sum(blk * blk, axis=-1, keepdims=True) + ss_rope
        r = lax.rsqrt(ss * (1.0 / MLA_QK) + EPS)
        k_ref[:, hd * LANES:(hd + 1) * LANES] = ((blk * gknn + kr_roped) * r).astype(BF16)
    vt = kvf[:, MLA_HEADS * LANES:].T.astype(BF16)
    ones_rows = jnp.where(lax.broadcasted_iota(jnp.int32, (BF16_SUBLANES, vt.shape[1]), 0) == 0,
                          1.0, 0.0).astype(BF16)
    for hd in range(MLA_HEADS):
        vt_ref[hd * VT_ROWS:hd * VT_ROWS + MLA_V, :] = vt[hd * MLA_V:(hd + 1) * MLA_V]
        vt_ref[hd * VT_ROWS + MLA_V:(hd + 1) * VT_ROWS, :] = ones_rows

    qk_r = _dot(h, win_ref[:, OFF_QK_R:OFF_V_R])
    half = RET_HEADS * RET_QK
    for j in range(half // LANES):
        qr_ref[:, j * LANES:(j + 1) * LANES] = rope_r(qk_r[:, j * LANES:(j + 1) * LANES]).astype(BF16)
        kb = qk_r[:, half + j * LANES:half + (j + 1) * LANES]
        kr_ref[:, j * LANES:(j + 1) * LANES] = (rope_r(kb) * (RET_QK ** -0.5)).astype(BF16)

    vr_ref[...] = _dot(h, win_ref[:, OFF_V_R:OFF_G_R]).astype(BF16)
    g = _dot(h, win_ref[:, OFF_G_R:OFF_GATE])
    sg_ref[...] = (g * jax.nn.sigmoid(g)).astype(BF16)
    gate_ref[...] = jax.nn.sigmoid(_dot(h, win_ref[:, OFF_GATE:OFF_END])).astype(BF16)


def _proj_call(x2, pos2, gmix, wsm, win, gqa, wqb, gkva, wkvb, gqn, gknn, gknr, invm, invr):
    t = x2.shape[0]
    tm = TM_PROJ
    row = lambda w: pl.BlockSpec((tm, w), lambda i: (i, 0))
    out_shape = (
        jax.ShapeDtypeStruct((t, MLA_HEADS * LANES), BF16),
        jax.ShapeDtypeStruct((t, MLA_HEADS * LANES), BF16),
        jax.ShapeDtypeStruct((MLA_HEADS * VT_ROWS, t), BF16),
        jax.ShapeDtypeStruct((t, RET_HEADS * RET_QK), BF16),
        jax.ShapeDtypeStruct((t, RET_HEADS * RET_QK), BF16),
        jax.ShapeDtypeStruct((t, RET_HEADS * RET_V), BF16),
        jax.ShapeDtypeStruct((t, RET_HEADS * RET_V), BF16),
        jax.ShapeDtypeStruct((t, 2 * D_MODEL), BF16),
    )
    return pl.pallas_call(
        _proj_kernel,
        grid=(t // tm,),
        in_specs=[
            row(D_MODEL), row(1),
            _const_spec(gmix.shape), _const_spec(wsm.shape), _const_spec(win.shape),
            _const_spec(gqa.shape),
            _const_spec(wqb.shape), _const_spec(gkva.shape), _const_spec(wkvb.shape),
            _const_spec(gqn.shape), _const_spec(gknn.shape), _const_spec(gknr.shape),
            _const_spec(invm.shape), _const_spec(invr.shape),
        ],
        out_specs=(
            row(MLA_HEADS * LANES), row(MLA_HEADS * LANES),
            pl.BlockSpec((MLA_HEADS * VT_ROWS, tm), lambda i: (0, i)),
            row(RET_HEADS * RET_QK), row(RET_HEADS * RET_QK),
            row(RET_HEADS * RET_V), row(RET_HEADS * RET_V), row(2 * D_MODEL),
        ),
        out_shape=out_shape,
        compiler_params=pltpu.CompilerParams(
            dimension_semantics=("arbitrary",), vmem_limit_bytes=VMEM_LIMIT_BYTES),
        name="proj_in",
    )(x2, pos2, gmix, wsm, win, gqa, wqb, gkva, wkvb, gqn, gknn, gknr, invm, invr)


def _attn_kernel(q_ref, k_ref, vt_ref, o_ref, s_ref, *, seq):
    n_q = seq // TQ
    n_k = seq // TK
    assert n_k % S_SLOTS == 0 and n_k > S_AHEAD

    def q_transposed(qi):
        q0 = pl.multiple_of(qi * TQ, TQ)
        return q_ref[pl.ds(q0, TQ), :].astype(F32).T.astype(BF16)

    def scores(qt, ki):
        s = _dot(k_ref[ki * TK:(ki + 1) * TK, :], qt)
        s_ref[ki % S_SLOTS] = s
        return jnp.max(s, axis=0, keepdims=True)

    def q_tile(qi, carry):
        qt, cms = carry
        qt_next = q_transposed(jnp.minimum(qi + 1, n_q - 1))
        m, cms = cms[0], cms[1:]
        acc = jnp.zeros((VT_ROWS, TQ), F32)
        for ki in range(n_k):
            ka = ki + S_AHEAD
            cms = cms + (scores(qt, ka) if ka < n_k else scores(qt_next, ka - n_k),)
            p = jnp.exp2(s_ref[ki % S_SLOTS] - m).astype(BF16)
            acc = acc + _dot(vt_ref[:, ki * TK:(ki + 1) * TK], p)
            if ki + 1 < n_k:
                m_new = jnp.maximum(m, cms[0])
                acc = acc * jnp.exp2(m - m_new)
                m, cms = m_new, cms[1:]
        q0 = pl.multiple_of(qi * TQ, TQ)
        o_ref[:, pl.ds(q0, TQ)] = (acc[:MLA_V] / acc[MLA_V:MLA_V + 1]).astype(BF16)
        return qt_next, cms

    qt0 = q_transposed(0)
    lax.fori_loop(0, n_q, q_tile, (qt0, tuple(scores(qt0, i) for i in range(S_AHEAD))))


def _attn_call(q, k, vt, batch, seq):
    t = batch * seq
    return pl.pallas_call(
        functools.partial(_attn_kernel, seq=seq),
        grid=(batch, MLA_HEADS),
        in_specs=[
            pl.BlockSpec((seq, LANES), lambda b, h: (b, h)),
            pl.BlockSpec((seq, LANES), lambda b, h: (b, h)),
            pl.BlockSpec((VT_ROWS, seq), lambda b, h: (h, b)),
        ],
        out_specs=pl.BlockSpec((MLA_V, seq), lambda b, h: (h, b)),
        out_shape=jax.ShapeDtypeStruct((MLA_HEADS * MLA_V, t), BF16),
        scratch_shapes=[pltpu.VMEM((S_SLOTS, TK, TQ), F32)],
        compiler_params=pltpu.CompilerParams(
            dimension_semantics=("arbitrary", "arbitrary"), vmem_limit_bytes=VMEM_LIMIT_BYTES),
        name="mla_attention",
    )(q, k, vt)


def _ret_kernel(dec_ref, q_ref, k_ref, v_ref, sg_ref, o_ref, kv_ref, st_ref, a_ref, *, seq):
    c = RET_C
    n = seq // c
    hd = pl.program_id(1)
    lo = (hd % 2) * RET_QK
    lane = lax.broadcasted_iota(jnp.int32, (1, LANES), 1)
    own = (lane >= lo) & (lane < lo + RET_QK)
    row = lax.broadcasted_iota(jnp.int32, (LANES, 1), 0)
    own_row = (row >= lo) & (row < lo + RET_QK)
    own_bf = jnp.where(own, 1.0, 0.0).astype(BF16)

    lg_f = -jnp.exp(jnp.full((1, 1), dec_ref[0, hd], F32))
    lg_b = -jnp.exp(jnp.full((1, 1), dec_ref[1, hd], F32))
    a = lax.broadcasted_iota(jnp.int32, (c, 1), 0).astype(F32)
    b = lax.broadcasted_iota(jnp.int32, (1, c), 1).astype(F32)
    q_decay = jnp.where(own, jnp.exp(lg_f * (a + 1.0)), jnp.exp(lg_b * (c - a)))
    k_decay = jnp.where(own, jnp.exp(lg_f * (c - 1.0 - a)), jnp.exp(lg_b * a))
    cd_f = jnp.exp(lg_f * float(c))
    cd_b = jnp.exp(lg_b * float(c))
    diff = a - b
    decay = jnp.where(diff >= 0, jnp.exp(lg_f * jnp.maximum(diff, 0.0)),
                      jnp.exp(lg_b * jnp.maximum(-diff, 0.0)))

    def both_halves(v):
        return jnp.where(own, v, pltpu.roll(v, RET_QK, 1))

    def chunk_kv(i, carry):
        r0 = pl.multiple_of(i * c, c)
        k = k_ref[pl.ds(r0, c), :].astype(F32)
        kv_ref[i] = _dot((both_halves(k) * k_decay).T.astype(BF16), v_ref[pl.ds(r0, c), :])
        return carry

    lax.fori_loop(0, n, chunk_kv, 0, unroll=4)

    def fwd_scan(i, f):
        st_ref[i] = f.astype(BF16)
        return f * cd_f + kv_ref[i]

    lax.fori_loop(0, n, fwd_scan, jnp.zeros((LANES, RET_V), F32))

    def bwd_scan(j, s):
        i = n - 1 - j
        st_ref[i] = jnp.where(own_row, st_ref[i].astype(F32), s).astype(BF16)
        return s * cd_b + kv_ref[i]

    lax.fori_loop(0, n, bwd_scan, jnp.zeros((LANES, RET_V), F32))

    def scores(i):
        rows = slice(i * c, (i + 1) * c)
        a_ref[i % RET_SLOTS] = lax.dot_general(
            q_ref[rows, :] * own_bf, k_ref[rows, :], (((1,), (1,)), ((), ())),
            preferred_element_type=F32)

    for i in range(min(RET_AHEAD, n)):
        scores(i)
    for i in range(n):
        if i + RET_AHEAD < n:
            scores(i + RET_AHEAD)
        rows = slice(i * c, (i + 1) * c)
        p = (a_ref[i % RET_SLOTS] * decay).astype(BF16)
        qd = (both_halves(q_ref[rows, :].astype(F32)) * q_decay).astype(BF16)
        o = _dot(p, v_ref[rows, :]) + _dot(qd, st_ref[i])
        o = o * _rms_scale(o, RET_V)
        o_ref[rows, :] = (sg_ref[rows, :].astype(F32) * o).astype(BF16)


def _ret_call(dec, qr, kr, vr, sg, batch, seq):
    t = batch * seq
    pair = pl.BlockSpec((seq, LANES), lambda b, h: (b, h // 2))
    head = pl.BlockSpec((seq, LANES), lambda b, h: (b, h))
    return pl.pallas_call(
        functools.partial(_ret_kernel, seq=seq),
        grid=(batch, RET_HEADS),
        in_specs=[pl.BlockSpec(memory_space=pltpu.SMEM), pair, pair, head, head],
        out_specs=head,
        out_shape=jax.ShapeDtypeStruct((t, RET_HEADS * RET_V), BF16),
        scratch_shapes=[pltpu.VMEM((seq // RET_C, LANES, RET_V), F32),
                        pltpu.VMEM((seq // RET_C, LANES, RET_V), BF16),
                        pltpu.VMEM((RET_SLOTS, RET_C, RET_C), F32)],
        compiler_params=pltpu.CompilerParams(
            dimension_semantics=("arbitrary", "arbitrary"), vmem_limit_bytes=VMEM_LIMIT_BYTES),
        name="retention",
    )(dec, qr, kr, vr, sg)


def _out_kernel(x_ref, ot_ref, ob_ref, gate_ref, wa_ref, wb_ref, wo_ref, gffn_ref, wgu_ref, wd_ref,
                y_ref, act_ref):
    o_a = ot_ref[...].astype(F32).T.astype(BF16)
    y_a = _dot(o_a, wa_ref[...])
    y_b = _dot(ob_ref[...], wb_ref[...])
    merged = (gate_ref[:, :D_MODEL].astype(F32) * y_a
              + gate_ref[:, D_MODEL:].astype(F32) * y_b).astype(BF16)
    x1 = x_ref[...] + _dot(merged, wo_ref[...])
    h2 = (x1 * _rms_scale(x1, D_MODEL) * gffn_ref[...]).astype(BF16)
    for cidx in range(FFN_HIDDEN // FFN_CHUNK):
        c0 = cidx * FFN_CHUNK
        gate = _dot(h2, wgu_ref[:, c0:c0 + FFN_CHUNK])
        up = _dot(h2, wgu_ref[:, FFN_HIDDEN + c0:FFN_HIDDEN + c0 + FFN_CHUNK])
        act_ref[:, c0:c0 + FFN_CHUNK] = (gate * jax.nn.sigmoid(gate) * up).astype(BF16)
    y_ref[...] = x1 + _dot(act_ref[...], wd_ref[...])


def _out_call(x2, ot, ob, gates, wa, wb, wo, gffn, wgu, wd):
    t = x2.shape[0]
    tm = TM_OUT
    row = lambda w: pl.BlockSpec((tm, w), lambda i: (i, 0))
    return pl.pallas_call(
        _out_kernel,
        grid=(t // tm,),
        in_specs=[
            row(D_MODEL),
            pl.BlockSpec((MLA_HEADS * MLA_V, tm), lambda i: (0, i)),
            row(RET_HEADS * RET_V), row(2 * D_MODEL),
            _const_spec(wa.shape), _const_spec(wb.shape), _const_spec(wo.shape),
            _const_spec(gffn.shape), _const_spec(wgu.shape), _const_spec(wd.shape),
        ],
        out_specs=row(D_MODEL),
        out_shape=jax.ShapeDtypeStruct((t, D_MODEL), F32),
        scratch_shapes=[pltpu.VMEM((tm, FFN_HIDDEN), BF16)],
        compiler_params=pltpu.CompilerParams(
            dimension_semantics=("arbitrary",), vmem_limit_bytes=VMEM_LIMIT_BYTES),
        name="merge_ffn",
    )(x2, ot, ob, gates, wa, wb, wo, gffn, wgu, wd)


def _pad_cols(w, left, right):
    return jnp.pad(w, ((0, 0), (left, right)))


def _layout_w_in(w_in):
    latent = MLA_Q_RANK + MLA_KV_RANK
    k_rope_blk = _pad_cols(w_in[:, latent:latent + MLA_ROPE], MLA_NOPE, LANES - MLA_QK)
    small = jnp.concatenate([w_in[:, :latent], k_rope_blk], axis=1).astype(BF16)
    return small, w_in[:, latent + MLA_ROPE:].astype(BF16)


def _layout_w_q_b(w_q_b):
    w = w_q_b.reshape(MLA_Q_RANK, MLA_HEADS, MLA_QK)
    w = jnp.pad(w, ((0, 0), (0, 0), (0, LANES - MLA_QK)))
    return w.reshape(MLA_Q_RANK, MLA_HEADS * LANES).astype(BF16)


def _layout_w_kv_b(w_kv_b):
    w = w_kv_b.reshape(MLA_KV_RANK, MLA_HEADS, MLA_NOPE + MLA_V)
    k_nope = jnp.pad(w[:, :, :MLA_NOPE], ((0, 0), (0, 0), (0, LANES - MLA_NOPE)))
    v = w[:, :, MLA_NOPE:]
    return jnp.concatenate([k_nope.reshape(MLA_KV_RANK, MLA_HEADS * LANES),
                            v.reshape(MLA_KV_RANK, MLA_HEADS * MLA_V)], axis=1).astype(BF16)


def _rope_freqs():
    lane = np.arange(LANES)
    half_m = MLA_ROPE // 2
    inv_m = np.zeros((1, LANES), np.float64)
    rope_lanes = (lane >= MLA_NOPE) & (lane < MLA_QK)
    idx_m = (lane - MLA_NOPE) % half_m
    inv_m[0, rope_lanes] = ROPE_THETA ** (-idx_m[rope_lanes] / half_m)
    half_r = RET_QK // 2
    inv_r = (ROPE_THETA ** (-(lane % half_r) / half_r)).reshape(1, LANES)
    return jnp.asarray(inv_m, F32), jnp.asarray(inv_r, F32)


def kernel(x, positions, g_mix, w_in, g_q_a, w_q_b, g_kv_a, w_kv_b, g_qn, g_kn, w_mla_out,
           ret_decay_fwd, ret_decay_bwd, w_ret_out, w_out, g_ffn, w_gate_up, w_down):
    batch, seq, d = x.shape
    t = batch * seq
    x2 = x.reshape(t, d)
    pos2 = positions.reshape(t, 1).astype(jnp.int32)
    inv_m, inv_r = _rope_freqs()

    row = lambda v: v.astype(F32).reshape(1, -1)
    gqn = jnp.pad(row(g_qn), ((0, 0), (0, LANES - MLA_QK))) * (MLA_QK ** -0.5 * math.log2(math.e))
    gknn = jnp.pad(row(g_kn)[:, :MLA_NOPE], ((0, 0), (0, LANES - MLA_NOPE)))
    gknr = jnp.pad(row(g_kn)[:, MLA_NOPE:], ((0, 0), (MLA_NOPE, LANES - MLA_QK)))

    q, k, vt, qr, kr, vr, sg, gates = _proj_call(
        x2, pos2, row(g_mix), *_layout_w_in(w_in), row(g_q_a), _layout_w_q_b(w_q_b),
        row(g_kv_a), _layout_w_kv_b(w_kv_b), gqn, gknn, gknr, inv_m, inv_r)

    ot = _attn_call(q, k, vt, batch, seq)
    dec = jnp.stack([ret_decay_fwd, ret_decay_bwd]).astype(F32)
    ob = _ret_call(dec, qr, kr, vr, sg, batch, seq)

    y = _out_call(x2, ot, ob, gates, w_mla_out.astype(BF16), w_ret_out.astype(BF16),
                  w_out.astype(BF16), row(g_ffn), w_gate_up.astype(BF16), w_down.astype(BF16))
    return y.reshape(batch, seq, d)
```

```python
import functools
import math

import numpy as np
import jax
import jax.numpy as jnp
from jax import lax
from jax.experimental import pallas as pl
from jax.experimental.pallas import tpu as pltpu

F32 = jnp.float32
BF16 = jnp.bfloat16

LANES = 128
VMEM_LIMIT_BYTES = 56 * 1024 * 1024

D_MODEL = 1024
MLA_HEADS = 8
MLA_Q_RANK = 256
MLA_KV_RANK = 128
MLA_NOPE = 64
MLA_ROPE = 32
MLA_V = 64
MLA_QK = MLA_NOPE + MLA_ROPE
RET_HEADS = 8
RET_QK = 64
RET_V = 128
FFN_HIDDEN = 2816
ROPE_THETA = 10000.0
EPS = 1e-6

OFF_QK_R = 0
OFF_V_R = OFF_QK_R + 2 * RET_HEADS * RET_QK
OFF_G_R = OFF_V_R + RET_HEADS * RET_V
OFF_GATE = OFF_G_R + RET_HEADS * RET_V
OFF_END = OFF_GATE + 2 * D_MODEL

TM_PROJ = 512
TM_OUT = 256
TQ = 512
TK = 256
S_AHEAD = 2
S_SLOTS = 4
BF16_SUBLANES = 16
VT_ROWS = MLA_V + BF16_SUBLANES
RET_C = 256
RET_AHEAD = 2
RET_SLOTS = 4
FFN_CHUNK = 256


def _dot(a, b):
    return jnp.dot(a, b, preferred_element_type=F32)


def _rms_scale(v, n):
    return lax.rsqrt(jnp.sum(v * v, axis=-1, keepdims=True) * (1.0 / n) + EPS)


def _const_spec(shape):
    zeros = (0,) * len(shape)
    return pl.BlockSpec(shape, lambda *_: zeros, pipeline_mode=pl.Buffered(1))


def _proj_kernel(x_ref, pos_ref, gmix_ref, wsm_ref, win_ref, gqa_ref, wqb_ref, gkva_ref, wkvb_ref,
                 gqn_ref, gknn_ref, gknr_ref, invm_ref, invr_ref,
                 q_ref, k_ref, vt_ref, qr_ref, kr_ref, vr_ref, gr_ref, gate_ref):
    x = x_ref[...]
    h = (x * _rms_scale(x, D_MODEL) * gmix_ref[...]).astype(BF16)

    pos = pos_ref[...].astype(F32)
    lane = lax.broadcasted_iota(jnp.int32, (1, LANES), 1)
    ang = pos * invm_ref[...]
    cs_m = jnp.cos(ang)
    sn = jnp.sin(ang)
    lo_m = jnp.where(lane < MLA_NOPE + MLA_ROPE // 2, -sn, 0.0)
    hi_m = jnp.where(lane >= MLA_NOPE + MLA_ROPE // 2, sn, 0.0)
    ang = pos * invr_ref[...]
    cs_r = jnp.cos(ang)
    sn = jnp.sin(ang)
    first_half = (lane % RET_QK) < RET_QK // 2
    lo_r = jnp.where(first_half, -sn, 0.0)
    hi_r = jnp.where(first_half, 0.0, sn)

    def rope_m(v):
        return (v * cs_m + pltpu.roll(v, LANES - MLA_ROPE // 2, 1) * lo_m
                + pltpu.roll(v, MLA_ROPE // 2, 1) * hi_m)

    def rope_r(v):
        return (v * cs_r + pltpu.roll(v, LANES - RET_QK // 2, 1) * lo_r
                + pltpu.roll(v, RET_QK // 2, 1) * hi_r)

    small = _dot(h, wsm_ref[...])
    cq = small[:, :MLA_Q_RANK]
    ckv = small[:, MLA_Q_RANK:MLA_Q_RANK + MLA_KV_RANK]
    krb = small[:, MLA_Q_RANK + MLA_KV_RANK:]

    cqn = (cq * _rms_scale(cq, MLA_Q_RANK) * gqa_ref[...]).astype(BF16)
    qf = _dot(cqn, wqb_ref[...])
    gqn = gqn_ref[...]
    for hd in range(MLA_HEADS):
        blk = qf[:, hd * LANES:(hd + 1) * LANES]
        qn = blk * _rms_scale(blk, MLA_QK) * gqn
        q_ref[:, hd * LANES:(hd + 1) * LANES] = rope_m(qn).astype(BF16)

    ckvn = (ckv * _rms_scale(ckv, MLA_KV_RANK) * gkva_ref[...]).astype(BF16)
    kvf = _dot(ckvn, wkvb_ref[...])
    kr_roped = rope_m(krb * gknr_ref[...])
    ss_rope = jnp.sum(krb * krb, axis=-1, keepdims=True)
    gknn = gknn_ref[...]
    for hd in range(MLA_HEADS):
        blk = kvf[:, hd * LANES:(hd + 1) * LANES]
        ss = jnp.sum(blk * blk, axis=-1, keepdims=True) + ss_rope
        r = lax.rsqrt(ss * (1.0 / MLA_QK) + EPS)
        k_ref[:, hd * LANES:(hd + 1) * LANES] = ((blk * gknn + kr_roped) * r).astype(BF16)
    vt = kvf[:, MLA_HEADS * LANES:].T.astype(BF16)
    ones_rows = jnp.where(lax.broadcasted_iota(jnp.int32, (BF16_SUBLANES, vt.shape[1]), 0) == 0,
                          1.0, 0.0).astype(BF16)
    for hd in range(MLA_HEADS):
        vt_ref[hd * VT_ROWS:hd * VT_ROWS + MLA_V, :] = vt[hd * MLA_V:(hd + 1) * MLA_V]
        vt_ref[hd * VT_ROWS + MLA_V:(hd + 1) * VT_ROWS, :] = ones_rows

    qk_r = _dot(h, win_ref[:, OFF_QK_R:OFF_V_R])
    half = RET_HEADS * RET_QK
    for j in range(half // LANES):
        qr_ref[:, j * LANES:(j + 1) * LANES] = rope_r(qk_r[:, j * LANES:(j + 1) * LANES]).astype(BF16)
        kb = qk_r[:, half + j * LANES:half + (j + 1) * LANES]
        kr_ref[:, j * LANES:(j + 1) * LANES] = (rope_r(kb) * (RET_QK ** -0.5)).astype(BF16)

    vr_ref[...] = _dot(h, win_ref[:, OFF_V_R:OFF_G_R]).astype(BF16)
    gr_ref[...] = _dot(h, win_ref[:, OFF_G_R:OFF_GATE]).astype(BF16)
    gate_ref[...] = _dot(h, win_ref[:, OFF_GATE:OFF_END]).astype(BF16)


def _proj_call(x2, pos2, gmix, wsm, win, gqa, wqb, gkva, wkvb, gqn, gknn, gknr, invm, invr):
    t = x2.shape[0]
    tm = TM_PROJ
    row = lambda w: pl.BlockSpec((tm, w), lambda i: (i, 0))
    out_shape = (
        jax.ShapeDtypeStruct((t, MLA_HEADS * LANES), BF16),
        jax.ShapeDtypeStruct((t, MLA_HEADS * LANES), BF16),
        jax.ShapeDtypeStruct((MLA_HEADS * VT_ROWS, t), BF16),
        jax.ShapeDtypeStruct((t, RET_HEADS * RET_QK), BF16),
        jax.ShapeDtypeStruct((t, RET_HEADS * RET_QK), BF16),
        jax.ShapeDtypeStruct((t, RET_HEADS * RET_V), BF16),
        jax.ShapeDtypeStruct((t, RET_HEADS * RET_V), BF16),
        jax.ShapeDtypeStruct((t, 2 * D_MODEL), BF16),
    )
    return pl.pallas_call(
        _proj_kernel,
        grid=(t // tm,),
        in_specs=[
            row(D_MODEL), row(1),
            _const_spec(gmix.shape), _const_spec(wsm.shape), _const_spec(win.shape),
            _const_spec(gqa.shape),
            _const_spec(wqb.shape), _const_spec(gkva.shape), _const_spec(wkvb.shape),
            _const_spec(gqn.shape), _const_spec(gknn.shape), _const_spec(gknr.shape),
            _const_spec(invm.shape), _const_spec(invr.shape),
        ],
        out_specs=(
            row(MLA_HEADS * LANES), row(MLA_HEADS * LANES),
            pl.BlockSpec((MLA_HEADS * VT_ROWS, tm), lambda i: (0, i)),
            row(RET_HEADS * RET_QK), row(RET_HEADS * RET_QK),
            row(RET_HEADS * RET_V), row(RET_HEADS * RET_V), row(2 * D_MODEL),
        ),
        out_shape=out_shape,
        compiler_params=pltpu.CompilerParams(
            dimension_semantics=("arbitrary",), vmem_limit_bytes=VMEM_LIMIT_BYTES),
        name="proj_in",
    )(x2, pos2, gmix, wsm, win, gqa, wqb, gkva, wkvb, gqn, gknn, gknr, invm, invr)


def _attn_kernel(q_ref, k_ref, vt_ref, o_ref, s_ref, *, seq):
    n_q = seq // TQ
    n_k = seq // TK
    assert n_k % S_SLOTS == 0 and n_k > S_AHEAD

    def q_transposed(qi):
        q0 = pl.multiple_of(qi * TQ, TQ)
        return q_ref[pl.ds(q0, TQ), :].astype(F32).T.astype(BF16)

    def scores(qt, ki):
        s = _dot(k_ref[ki * TK:(ki + 1) * TK, :], qt)
        s_ref[ki % S_SLOTS] = s
        return jnp.max(s, axis=0, keepdims=True)

    def q_tile(qi, carry):
        qt, cms = carry
        qt_next = q_transposed(jnp.minimum(qi + 1, n_q - 1))
        m, cms = cms[0], cms[1:]
        acc = jnp.zeros((VT_ROWS, TQ), F32)
        for ki in range(n_k):
            ka = ki + S_AHEAD
            cms = cms + (scores(qt, ka) if ka < n_k else scores(qt_next, ka - n_k),)
            p = jnp.exp2(s_ref[ki % S_SLOTS] - m).astype(BF16)
            acc = acc + _dot(vt_ref[:, ki * TK:(ki + 1) * TK], p)
            if ki + 1 < n_k:
                m_new = jnp.maximum(m, cms[0])
                acc = acc * jnp.exp2(m - m_new)
                m, cms = m_new, cms[1:]
        q0 = pl.multiple_of(qi * TQ, TQ)
        o_ref[:, pl.ds(q0, TQ)] = (acc[:MLA_V] / acc[MLA_V:MLA_V + 1]).astype(BF16)
        return qt_next, cms

    qt0 = q_transposed(0)
    lax.fori_loop(0, n_q, q_tile, (qt0, tuple(scores(qt0, i) for i in range(S_AHEAD))))


def _attn_call(q, k, vt, batch, seq):
    t = batch * seq
    return pl.pallas_call(
        functools.partial(_attn_kernel, seq=seq),
        grid=(batch, MLA_HEADS),
        in_specs=[
            pl.BlockSpec((seq, LANES), lambda b, h: (b, h)),
            pl.BlockSpec((seq, LANES), lambda b, h: (b, h)),
            pl.BlockSpec((VT_ROWS, seq), lambda b, h: (h, b)),
        ],
        out_specs=pl.BlockSpec((MLA_V, seq), lambda b, h: (h, b)),
        out_shape=jax.ShapeDtypeStruct((MLA_HEADS * MLA_V, t), BF16),
        scratch_shapes=[pltpu.VMEM((S_SLOTS, TK, TQ), F32)],
        compiler_params=pltpu.CompilerParams(
            dimension_semantics=("arbitrary", "arbitrary"), vmem_limit_bytes=VMEM_LIMIT_BYTES),
        name="mla_attention",
    )(q, k, vt)


def _ret_kernel(dec_ref, q_ref, k_ref, v_ref, o_ref, kv_ref, st_ref, a_ref, *, seq):
    c = RET_C
    n = seq // c
    hd = pl.program_id(1)
    lo = (hd % 2) * RET_QK
    lane = lax.broadcasted_iota(jnp.int32, (1, LANES), 1)
    own = (lane >= lo) & (lane < lo + RET_QK)
    row = lax.broadcasted_iota(jnp.int32, (LANES, 1), 0)
    own_row = (row >= lo) & (row < lo + RET_QK)
    own_bf = jnp.where(own, 1.0, 0.0).astype(BF16)

    lg_f = -jnp.exp(jnp.full((1, 1), dec_ref[0, hd], F32))
    lg_b = -jnp.exp(jnp.full((1, 1), dec_ref[1, hd], F32))
    a = lax.broadcasted_iota(jnp.int32, (c, 1), 0).astype(F32)
    b = lax.broadcasted_iota(jnp.int32, (1, c), 1).astype(F32)
    q_decay = jnp.where(own, jnp.exp(lg_f * (a + 1.0)), jnp.exp(lg_b * (c - a)))
    k_decay = jnp.where(own, jnp.exp(lg_f * (c - 1.0 - a)), jnp.exp(lg_b * a))
    cd_f = jnp.exp(lg_f * float(c))
    cd_b = jnp.exp(lg_b * float(c))
    diff = a - b
    decay = jnp.where(diff >= 0, jnp.exp(lg_f * jnp.maximum(diff, 0.0)),
                      jnp.exp(lg_b * jnp.maximum(-diff, 0.0)))

    def both_halves(v):
        return jnp.where(own, v, pltpu.roll(v, RET_QK, 1))

    def chunk_kv(i, carry):
        r0 = pl.multiple_of(i * c, c)
        k = k_ref[pl.ds(r0, c), :].astype(F32)
        kv_ref[i] = _dot((both_halves(k) * k_decay).T.astype(BF16), v_ref[pl.ds(r0, c), :])
        return carry

    lax.fori_loop(0, n, chunk_kv, 0, unroll=True)

    def fwd_scan(i, f):
        st_ref[i] = f.astype(BF16)
        return f * cd_f + kv_ref[i]

    lax.fori_loop(0, n, fwd_scan, jnp.zeros((LANES, RET_V), F32))

    def bwd_scan(j, s):
        i = n - 1 - j
        st_ref[i] = jnp.where(own_row, st_ref[i].astype(F32), s).astype(BF16)
        return s * cd_b + kv_ref[i]

    lax.fori_loop(0, n, bwd_scan, jnp.zeros((LANES, RET_V), F32))

    def scores(i):
        rows = slice(i * c, (i + 1) * c)
        a_ref[i % RET_SLOTS] = lax.dot_general(
            q_ref[rows, :] * own_bf, k_ref[rows, :], (((1,), (1,)), ((), ())),
            preferred_element_type=F32)

    for i in range(min(RET_AHEAD, n)):
        scores(i)
    for i in range(n):
        if i + RET_AHEAD < n:
            scores(i + RET_AHEAD)
        rows = slice(i * c, (i + 1) * c)
        p = (a_ref[i % RET_SLOTS] * decay).astype(BF16)
        qd = (both_halves(q_ref[rows, :].astype(F32)) * q_decay).astype(BF16)
        o = _dot(p, v_ref[rows, :]) + _dot(qd, st_ref[i])
        o_ref[rows, :] = (o * _rms_scale(o, RET_V)).astype(BF16)


def _ret_call(dec, qr, kr, vr, batch, seq):
    t = batch * seq
    pair = pl.BlockSpec((seq, LANES), lambda b, h: (b, h // 2))
    head = pl.BlockSpec((seq, LANES), lambda b, h: (b, h))
    return pl.pallas_call(
        functools.partial(_ret_kernel, seq=seq),
        grid=(batch, RET_HEADS),
        in_specs=[pl.BlockSpec(memory_space=pltpu.SMEM), pair, pair, head],
        out_specs=head,
        out_shape=jax.ShapeDtypeStruct((t, RET_HEADS * RET_V), BF16),
        scratch_shapes=[pltpu.VMEM((seq // RET_C, LANES, RET_V), F32),
                        pltpu.VMEM((seq // RET_C, LANES, RET_V), BF16),
                        pltpu.VMEM((RET_SLOTS, RET_C, RET_C), F32)],
        compiler_params=pltpu.CompilerParams(
            dimension_semantics=("arbitrary", "arbitrary"), vmem_limit_bytes=VMEM_LIMIT_BYTES),
        name="retention",
    )(dec, qr, kr, vr)


def _out_kernel(x_ref, ot_ref, ret_ref, gr_ref, gate_ref, wa_ref, wb_ref, wo_ref, gffn_ref,
                wgu_ref, wd_ref, y_ref, act_ref):
    o_a = ot_ref[...].astype(F32).T.astype(BF16)
    y_a = _dot(o_a, wa_ref[...])
    g = gr_ref[...].astype(F32)
    o_b = (g * jax.nn.sigmoid(g) * ret_ref[...].astype(F32)).astype(BF16)
    y_b = _dot(o_b, wb_ref[...])
    merged = (jax.nn.sigmoid(gate_ref[:, :D_MODEL].astype(F32)) * y_a
              + jax.nn.sigmoid(gate_ref[:, D_MODEL:].astype(F32)) * y_b).astype(BF16)
    x1 = x_ref[...] + _dot(merged, wo_ref[...])
    h2 = (x1 * _rms_scale(x1, D_MODEL) * gffn_ref[...]).astype(BF16)
    for cidx in range(FFN_HIDDEN // FFN_CHUNK):
        c0 = cidx * FFN_CHUNK
        gate = _dot(h2, wgu_ref[:, c0:c0 + FFN_CHUNK])
        up = _dot(h2, wgu_ref[:, FFN_HIDDEN + c0:FFN_HIDDEN + c0 + FFN_CHUNK])
        act_ref[:, c0:c0 + FFN_CHUNK] = (gate * jax.nn.sigmoid(gate) * up).astype(BF16)
    y_ref[...] = x1 + _dot(act_ref[...], wd_ref[...])


def _out_call(x2, ot, ret, gr, gates, wa, wb, wo, gffn, wgu, wd):
    t = x2.shape[0]
    tm = TM_OUT
    row = lambda w: pl.BlockSpec((tm, w), lambda i: (i, 0))
    return pl.pallas_call(
        _out_kernel,
        grid=(t // tm,),
        in_specs=[
            row(D_MODEL),
            pl.BlockSpec((MLA_HEADS * MLA_V, tm), lambda i: (0, i)),
            row(RET_HEADS * RET_V), row(RET_HEADS * RET_V), row(2 * D_MODEL),
            _const_spec(wa.shape), _const_spec(wb.shape), _const_spec(wo.shape),
            _const_spec(gffn.shape), _const_spec(wgu.shape), _const_spec(wd.shape),
        ],
        out_specs=row(D_MODEL),
        out_shape=jax.ShapeDtypeStruct((t, D_MODEL), F32),
        scratch_shapes=[pltpu.VMEM((tm, FFN_HIDDEN), BF16)],
        compiler_params=pltpu.CompilerParams(
            dimension_semantics=("arbitrary",), vmem_limit_bytes=VMEM_LIMIT_BYTES),
        name="merge_ffn",
    )(x2, ot, ret, gr, gates, wa, wb, wo, gffn, wgu, wd)


def _pad_cols(w, left, right):
    return jnp.pad(w, ((0, 0), (left, right)))


def _layout_w_in(w_in):
    latent = MLA_Q_RANK + MLA_KV_RANK
    k_rope_blk = _pad_cols(w_in[:, latent:latent + MLA_ROPE], MLA_NOPE, LANES - MLA_QK)
    small = jnp.concatenate([w_in[:, :latent], k_rope_blk], axis=1).astype(BF16)
    return small, w_in[:, latent + MLA_ROPE:].astype(BF16)


def _layout_w_q_b(w_q_b):
    w = w_q_b.reshape(MLA_Q_RANK, MLA_HEADS, MLA_QK)
    w = jnp.pad(w, ((0, 0), (0, 0), (0, LANES - MLA_QK)))
    return w.reshape(MLA_Q_RANK, MLA_HEADS * LANES).astype(BF16)


def _layout_w_kv_b(w_kv_b):
    w = w_kv_b.reshape(MLA_KV_RANK, MLA_HEADS, MLA_NOPE + MLA_V)
    k_nope = jnp.pad(w[:, :, :MLA_NOPE], ((0, 0), (0, 0), (0, LANES - MLA_NOPE)))
    v = w[:, :, MLA_NOPE:]
    return jnp.concatenate([k_nope.reshape(MLA_KV_RANK, MLA_HEADS * LANES),
                            v.reshape(MLA_KV_RANK, MLA_HEADS * MLA_V)], axis=1).astype(BF16)


def _rope_freqs():
    lane = np.arange(LANES)
    half_m = MLA_ROPE // 2
    inv_m = np.zeros((1, LANES), np.float64)
    rope_lanes = (lane >= MLA_NOPE) & (lane < MLA_QK)
    idx_m = (lane - MLA_NOPE) % half_m
    inv_m[0, rope_lanes] = ROPE_THETA ** (-idx_m[rope_lanes] / half_m)
    half_r = RET_QK // 2
    inv_r = (ROPE_THETA ** (-(lane % half_r) / half_r)).reshape(1, LANES)
    return jnp.asarray(inv_m, F32), jnp.asarray(inv_r, F32)


def kernel(x, positions, g_mix, w_in, g_q_a, w_q_b, g_kv_a, w_kv_b, g_qn, g_kn, w_mla_out,
           ret_decay_fwd, ret_decay_bwd, w_ret_out, w_out, g_ffn, w_gate_up, w_down):
    batch, seq, d = x.shape
    t = batch * seq
    x2 = x.reshape(t, d)
    pos2 = positions.reshape(t, 1).astype(jnp.int32)
    inv_m, inv_r = _rope_freqs()

    row = lambda v: v.astype(F32).reshape(1, -1)
    gqn = jnp.pad(row(g_qn), ((0, 0), (0, LANES - MLA_QK))) * (MLA_QK ** -0.5 * math.log2(math.e))
    gknn = jnp.pad(row(g_kn)[:, :MLA_NOPE], ((0, 0), (0, LANES - MLA_NOPE)))
    gknr = jnp.pad(row(g_kn)[:, MLA_NOPE:], ((0, 0), (MLA_NOPE, LANES - MLA_QK)))

    q, k, vt, qr, kr, vr, gr, gates = _proj_call(
        x2, pos2, row(g_mix), *_layout_w_in(w_in), row(g_q_a), _layout_w_q_b(w_q_b),
        row(g_kv_a), _layout_w_kv_b(w_kv_b), gqn, gknn, gknr, inv_m, inv_r)

    ot = _attn_call(q, k, vt, batch, seq)
    dec = jnp.stack([ret_decay_fwd, ret_decay_bwd]).astype(F32)
    ret = _ret_call(dec, qr, kr, vr, batch, seq)

    y = _out_call(x2, ot, ret, gr, gates, w_mla_out.astype(BF16), w_ret_out.astype(BF16),
                  w_out.astype(BF16), row(g_ffn), w_gate_up.astype(BF16), w_down.astype(BF16))
    return y.reshape(batch, seq, d)
```

```python
import functools
import math

import numpy as np
import jax
import jax.numpy as jnp
from jax import lax
from jax.experimental import pallas as pl
from jax.experimental.pallas import tpu as pltpu

F32 = jnp.float32
BF16 = jnp.bfloat16

LANES = 128
VMEM_LIMIT_BYTES = 56 * 1024 * 1024

D_MODEL = 1024
MLA_HEADS = 8
MLA_Q_RANK = 256
MLA_KV_RANK = 128
MLA_NOPE = 64
MLA_ROPE = 32
MLA_V = 64
MLA_QK = MLA_NOPE + MLA_ROPE
RET_HEADS = 8
RET_QK = 64
RET_V = 128
FFN_HIDDEN = 2816
ROPE_THETA = 10000.0
EPS = 1e-6

W_IN_MLA_COLS = MLA_Q_RANK + MLA_KV_RANK + MLA_ROPE
OFF_QK_R = 0
OFF_V_R = OFF_QK_R + 2 * RET_HEADS * RET_QK
OFF_G_R = OFF_V_R + RET_HEADS * RET_V
OFF_GATE = OFF_G_R + RET_HEADS * RET_V
OFF_END = OFF_GATE + 2 * D_MODEL

TM_PROJ = 512
TM_OUT = 256
TQ = 512
TK = 256
S_AHEAD = 2
S_SLOTS = 4
Q_PER_BODY = 4
BF16_SUBLANES = 16
VT_ROWS = MLA_V + BF16_SUBLANES
RET_C = 256
RET_AHEAD = 2
RET_SLOTS = 4
FFN_CHUNK = 256


def _dot(a, b):
    return jnp.dot(a, b, preferred_element_type=F32)


def _rms_scale(v, n):
    return lax.rsqrt(jnp.sum(v * v, axis=-1, keepdims=True) * (1.0 / n) + EPS)


def _const_spec(shape):
    zeros = (0,) * len(shape)
    return pl.BlockSpec(shape, lambda *_: zeros, pipeline_mode=pl.Buffered(1))


def _proj_kernel(x_ref, pos_ref, gmix_ref, wsm_ref, win_ref, gqa_ref, wqb_ref, gkva_ref, wkvb_ref,
                 gqn_ref, gknn_ref, gknr_ref, inv_ref,
                 q_ref, k_ref, vt_ref, qr_ref, kr_ref, vr_ref, gr_ref, gate_ref, wal_ref):
    @pl.when(pl.program_id(0) == 0)
    def _():
        for j in range(OFF_END // LANES):
            c0 = W_IN_MLA_COLS + j * LANES
            wal_ref[:, j * LANES:(j + 1) * LANES] = win_ref[:, c0:c0 + LANES]

    x = x_ref[...]
    h = (x * _rms_scale(x, D_MODEL) * gmix_ref[...]).astype(BF16)

    pos = pos_ref[...].astype(F32)
    lane = lax.broadcasted_iota(jnp.int32, (1, LANES), 1)
    half = RET_QK // 2
    ang = pos * inv_ref[...]
    cs_r = jnp.cos(ang)
    sn = jnp.sin(ang)
    first_half = (lane % RET_QK) < half
    lo_r = jnp.where(first_half, -sn, 0.0)
    hi_r = jnp.where(first_half, 0.0, sn)
    mla_rope = ((lane % 2) == 0) & (lane < RET_QK)
    cs_m = jnp.where(mla_rope, cs_r, 1.0)
    lo_m = jnp.where(mla_rope, lo_r, 0.0)
    hi_m = jnp.where(mla_rope, hi_r, 0.0)

    def rope(v, cs, lo, hi):
        return (v * cs + pltpu.roll(v, LANES - half, 1) * lo + pltpu.roll(v, half, 1) * hi)

    rope_m = functools.partial(rope, cs=cs_m, lo=lo_m, hi=hi_m)
    rope_r = functools.partial(rope, cs=cs_r, lo=lo_r, hi=hi_r)

    small = _dot(h, wsm_ref[...])
    cq = small[:, :MLA_Q_RANK]
    ckv = small[:, MLA_Q_RANK:MLA_Q_RANK + MLA_KV_RANK]
    krb = small[:, MLA_Q_RANK + MLA_KV_RANK:]

    cqn = (cq * _rms_scale(cq, MLA_Q_RANK) * gqa_ref[...]).astype(BF16)
    qf = _dot(cqn, wqb_ref[...])
    gqn = gqn_ref[...]
    gate_ref[:, :D_MODEL] = _dot(h, wal_ref[:, OFF_GATE:OFF_GATE + D_MODEL]).astype(BF16)
    for hd in range(MLA_HEADS):
        blk = qf[:, hd * LANES:(hd + 1) * LANES]
        qn = blk * _rms_scale(blk, MLA_QK) * gqn
        q_ref[:, hd * LANES:(hd + 1) * LANES] = rope_m(qn).astype(BF16)

    ckvn = (ckv * _rms_scale(ckv, MLA_KV_RANK) * gkva_ref[...]).astype(BF16)
    kvf = _dot(ckvn, wkvb_ref[...])
    vr_ref[...] = _dot(h, wal_ref[:, OFF_V_R:OFF_G_R]).astype(BF16)
    kr_roped = rope_m(krb * gknr_ref[...])
    ss_rope = jnp.sum(krb * krb, axis=-1, keepdims=True)
    gknn = gknn_ref[...]
    for hd in range(MLA_HEADS):
        blk = kvf[:, hd * LANES:(hd + 1) * LANES]
        ss = jnp.sum(blk * blk, axis=-1, keepdims=True) + ss_rope
        r = lax.rsqrt(ss * (1.0 / MLA_QK) + EPS)
        k_ref[:, hd * LANES:(hd + 1) * LANES] = ((blk * gknn + kr_roped) * r).astype(BF16)
    gr_ref[...] = _dot(h, wal_ref[:, OFF_G_R:OFF_GATE]).astype(BF16)
    vt = kvf[:, MLA_HEADS * LANES:].T.astype(BF16)
    ones_rows = jnp.where(lax.broadcasted_iota(jnp.int32, (BF16_SUBLANES, vt.shape[1]), 0) == 0,
                          1.0, 0.0).astype(BF16)
    for hd in range(MLA_HEADS):
        vt_ref[hd * VT_ROWS:hd * VT_ROWS + MLA_V, :] = vt[hd * MLA_V:(hd + 1) * MLA_V]
        vt_ref[hd * VT_ROWS + MLA_V:(hd + 1) * VT_ROWS, :] = ones_rows

    qk_r = _dot(h, wal_ref[:, OFF_QK_R:OFF_V_R])
    gate_ref[:, D_MODEL:] = _dot(h, wal_ref[:, OFF_GATE + D_MODEL:OFF_END]).astype(BF16)
    k_off = RET_HEADS * RET_QK
    for j in range(k_off // LANES):
        qr_ref[:, j * LANES:(j + 1) * LANES] = rope_r(qk_r[:, j * LANES:(j + 1) * LANES]).astype(BF16)
        kb = qk_r[:, k_off + j * LANES:k_off + (j + 1) * LANES]
        kr_ref[:, j * LANES:(j + 1) * LANES] = (rope_r(kb) * (RET_QK ** -0.5)).astype(BF16)


def _proj_call(x2, pos2, gmix, wsm, win, gqa, wqb, gkva, wkvb, gqn, gknn, gknr, inv):
    t = x2.shape[0]
    tm = TM_PROJ
    row = lambda w: pl.BlockSpec((tm, w), lambda i: (i, 0))
    out_shape = (
        jax.ShapeDtypeStruct((t, MLA_HEADS * LANES), BF16),
        jax.ShapeDtypeStruct((t, MLA_HEADS * LANES), BF16),
        jax.ShapeDtypeStruct((MLA_HEADS * VT_ROWS, t), BF16),
        jax.ShapeDtypeStruct((t, RET_HEADS * RET_QK), BF16),
        jax.ShapeDtypeStruct((t, RET_HEADS * RET_QK), BF16),
        jax.ShapeDtypeStruct((t, RET_HEADS * RET_V), BF16),
        jax.ShapeDtypeStruct((t, RET_HEADS * RET_V), BF16),
        jax.ShapeDtypeStruct((t, 2 * D_MODEL), BF16),
    )
    return pl.pallas_call(
        _proj_kernel,
        grid=(t // tm,),
        in_specs=[
            row(D_MODEL), row(1),
            _const_spec(gmix.shape), _const_spec(wsm.shape), _const_spec(win.shape),
            _const_spec(gqa.shape),
            _const_spec(wqb.shape), _const_spec(gkva.shape), _const_spec(wkvb.shape),
            _const_spec(gqn.shape), _const_spec(gknn.shape), _const_spec(gknr.shape),
            _const_spec(inv.shape),
        ],
        out_specs=(
            row(MLA_HEADS * LANES), row(MLA_HEADS * LANES),
            pl.BlockSpec((MLA_HEADS * VT_ROWS, tm), lambda i: (0, i)),
            row(RET_HEADS * RET_QK), row(RET_HEADS * RET_QK),
            row(RET_HEADS * RET_V), row(RET_HEADS * RET_V), row(2 * D_MODEL),
        ),
        out_shape=out_shape,
        scratch_shapes=[pltpu.VMEM((D_MODEL, OFF_END), BF16)],
        compiler_params=pltpu.CompilerParams(
            dimension_semantics=("arbitrary",), vmem_limit_bytes=VMEM_LIMIT_BYTES),
        name="proj_in",
    )(x2, pos2, gmix, wsm, win, gqa, wqb, gkva, wkvb, gqn, gknn, gknr, inv)


def _attn_kernel(q_ref, k_ref, vt_ref, o_ref, s_ref, *, seq):
    n_q = seq // TQ
    n_k = seq // TK
    assert n_k % S_SLOTS == 0 and n_k > S_AHEAD

    def q_transposed(qi):
        q0 = pl.multiple_of(qi * TQ, TQ)
        return q_ref[pl.ds(q0, TQ), :].astype(F32).T.astype(BF16)

    def scores(qt, ki):
        s = _dot(k_ref[ki * TK:(ki + 1) * TK, :], qt)
        s_ref[ki % S_SLOTS] = s
        return jnp.max(s, axis=0, keepdims=True)

    def q_tile(qi, carry):
        qt, cms = carry
        qt_next = q_transposed(jnp.minimum(qi + 1, n_q - 1))
        m, cms = cms[0], cms[1:]
        acc = jnp.zeros((VT_ROWS, TQ), F32)
        for ki in range(n_k):
            ka = ki + S_AHEAD
            cms = cms + (scores(qt, ka) if ka < n_k else scores(qt_next, ka - n_k),)
            p = jnp.exp2(s_ref[ki % S_SLOTS] - m).astype(BF16)
            acc = acc + _dot(vt_ref[:, ki * TK:(ki + 1) * TK], p)
            if ki + 1 < n_k:
                m_new = jnp.maximum(m, cms[0])
                acc = acc * jnp.exp2(m - m_new)
                m, cms = m_new, cms[1:]
        q0 = pl.multiple_of(qi * TQ, TQ)
        o_ref[:, pl.ds(q0, TQ)] = (acc[:MLA_V] / acc[MLA_V:MLA_V + 1]).astype(BF16)
        return qt_next, cms

    def q_tiles(j, carry):
        for u in range(Q_PER_BODY):
            carry = q_tile(j * Q_PER_BODY + u, carry)
        return carry

    assert n_q % Q_PER_BODY == 0
    qt0 = q_transposed(0)
    lax.fori_loop(0, n_q // Q_PER_BODY, q_tiles,
                  (qt0, tuple(scores(qt0, i) for i in range(S_AHEAD))))


def _attn_call(q, k, vt, batch, seq):
    t = batch * seq
    return pl.pallas_call(
        functools.partial(_attn_kernel, seq=seq),
        grid=(batch, MLA_HEADS),
        in_specs=[
            pl.BlockSpec((seq, LANES), lambda b, h: (b, h)),
            pl.BlockSpec((seq, LANES), lambda b, h: (b, h)),
            pl.BlockSpec((VT_ROWS, seq), lambda b, h: (h, b)),
        ],
        out_specs=pl.BlockSpec((MLA_V, seq), lambda b, h: (h, b)),
        out_shape=jax.ShapeDtypeStruct((MLA_HEADS * MLA_V, t), BF16),
        scratch_shapes=[pltpu.VMEM((S_SLOTS, TK, TQ), F32)],
        compiler_params=pltpu.CompilerParams(
            dimension_semantics=("arbitrary", "arbitrary"), vmem_limit_bytes=VMEM_LIMIT_BYTES),
        name="mla_attention",
    )(q, k, vt)


def _ret_kernel(dec_ref, q_ref, k_ref, v_ref, o_ref, kv_ref, st_ref, a_ref, *, seq):
    c = RET_C
    n = seq // c
    hd = pl.program_id(1)
    lo = (hd % 2) * RET_QK
    lane = lax.broadcasted_iota(jnp.int32, (1, LANES), 1)
    own = (lane >= lo) & (lane < lo + RET_QK)
    row = lax.broadcasted_iota(jnp.int32, (LANES, 1), 0)
    own_row = (row >= lo) & (row < lo + RET_QK)
    own_bf = jnp.where(own, 1.0, 0.0).astype(BF16)

    lg_f = -jnp.exp(jnp.full((1, 1), dec_ref[0, hd], F32))
    lg_b = -jnp.exp(jnp.full((1, 1), dec_ref[1, hd], F32))
    a = lax.broadcasted_iota(jnp.int32, (c, 1), 0).astype(F32)
    b = lax.broadcasted_iota(jnp.int32, (1, c), 1).astype(F32)
    q_decay = jnp.where(own, jnp.exp(lg_f * (a + 1.0)), jnp.exp(lg_b * (c - a)))
    k_decay = jnp.where(own, jnp.exp(lg_f * (c - 1.0 - a)), jnp.exp(lg_b * a))
    cd_f = jnp.exp(lg_f * float(c))
    cd_b = jnp.exp(lg_b * float(c))
    diff = a - b
    decay = jnp.where(diff >= 0, jnp.exp(lg_f * jnp.maximum(diff, 0.0)),
                      jnp.exp(lg_b * jnp.maximum(-diff, 0.0)))

    def both_halves(v):
        return jnp.where(own, v, pltpu.roll(v, RET_QK, 1))

    def chunk_kv(i, carry):
        r0 = pl.multiple_of(i * c, c)
        k = k_ref[pl.ds(r0, c), :].astype(F32)
        kv_ref[i] = _dot((both_halves(k) * k_decay).T.astype(BF16), v_ref[pl.ds(r0, c), :])
        return carry

    lax.fori_loop(0, n, chunk_kv, 0, unroll=True)

    def fwd_scan(i, f):
        st_ref[i] = f.astype(BF16)
        return f * cd_f + kv_ref[i]

    lax.fori_loop(0, n, fwd_scan, jnp.zeros((LANES, RET_V), F32))

    def bwd_scan(j, s):
        i = n - 1 - j
        st_ref[i] = jnp.where(own_row, st_ref[i].astype(F32), s).astype(BF16)
        return s * cd_b + kv_ref[i]

    lax.fori_loop(0, n, bwd_scan, jnp.zeros((LANES, RET_V), F32))

    def scores(i):
        rows = slice(i * c, (i + 1) * c)
        a_ref[i % RET_SLOTS] = lax.dot_general(
            q_ref[rows, :] * own_bf, k_ref[rows, :], (((1,), (1,)), ((), ())),
            preferred_element_type=F32)

    for i in range(min(RET_AHEAD, n)):
        scores(i)
    for i in range(n):
        if i + RET_AHEAD < n:
            scores(i + RET_AHEAD)
        rows = slice(i * c, (i + 1) * c)
        p = (a_ref[i % RET_SLOTS] * decay).astype(BF16)
        qd = (both_halves(q_ref[rows, :].astype(F32)) * q_decay).astype(BF16)
        o = _dot(p, v_ref[rows, :]) + _dot(qd, st_ref[i])
        o_ref[rows, :] = (o * _rms_scale(o, RET_V)).astype(BF16)


def _ret_call(dec, qr, kr, vr, batch, seq):
    t = batch * seq
    pair = pl.BlockSpec((seq, LANES), lambda b, h: (b, h // 2))
    head = pl.BlockSpec((seq, LANES), lambda b, h: (b, h))
    return pl.pallas_call(
        functools.partial(_ret_kernel, seq=seq),
        grid=(batch, RET_HEADS),
        in_specs=[pl.BlockSpec(memory_space=pltpu.SMEM), pair, pair, head],
        out_specs=head,
        out_shape=jax.ShapeDtypeStruct((t, RET_HEADS * RET_V), BF16),
        scratch_shapes=[pltpu.VMEM((seq // RET_C, LANES, RET_V), F32),
                        pltpu.VMEM((seq // RET_C, LANES, RET_V), BF16),
                        pltpu.VMEM((RET_SLOTS, RET_C, RET_C), F32)],
        compiler_params=pltpu.CompilerParams(
            dimension_semantics=("arbitrary", "arbitrary"), vmem_limit_bytes=VMEM_LIMIT_BYTES),
        name="retention",
    )(dec, qr, kr, vr)


def _out_kernel(x_ref, ot_ref, ret_ref, gr_ref, gate_ref, wa_ref, wb_ref, wo_ref, gffn_ref,
                wgu_ref, wd_ref, y_ref, act_ref):
    o_a = ot_ref[...].astype(F32).T.astype(BF16)
    y_a = _dot(o_a, wa_ref[...])
    g = gr_ref[...].astype(F32)
    o_b = (g * jax.nn.sigmoid(g) * ret_ref[...].astype(F32)).astype(BF16)
    y_b = _dot(o_b, wb_ref[...])
    merged = (jax.nn.sigmoid(gate_ref[:, :D_MODEL].astype(F32)) * y_a
              + jax.nn.sigmoid(gate_ref[:, D_MODEL:].astype(F32)) * y_b).astype(BF16)
    x1 = x_ref[...] + _dot(merged, wo_ref[...])
    h2 = (x1 * _rms_scale(x1, D_MODEL) * gffn_ref[...]).astype(BF16)
    for cidx in range(FFN_HIDDEN // FFN_CHUNK):
        c0 = cidx * FFN_CHUNK
        gate = _dot(h2, wgu_ref[:, c0:c0 + FFN_CHUNK])
        up = _dot(h2, wgu_ref[:, FFN_HIDDEN + c0:FFN_HIDDEN + c0 + FFN_CHUNK])
        act_ref[:, c0:c0 + FFN_CHUNK] = (gate * jax.nn.sigmoid(gate) * up).astype(BF16)
    y_ref[...] = x1 + _dot(act_ref[...], wd_ref[...])


def _out_call(x2, ot, ret, gr, gates, wa, wb, wo, gffn, wgu, wd):
    t = x2.shape[0]
    tm = TM_OUT
    row = lambda w: pl.BlockSpec((tm, w), lambda i: (i, 0))
    return pl.pallas_call(
        _out_kernel,
        grid=(t // tm,),
        in_specs=[
            row(D_MODEL),
            pl.BlockSpec((MLA_HEADS * MLA_V, tm), lambda i: (0, i)),
            row(RET_HEADS * RET_V), row(RET_HEADS * RET_V), row(2 * D_MODEL),
            _const_spec(wa.shape), _const_spec(wb.shape), _const_spec(wo.shape),
            _const_spec(gffn.shape), _const_spec(wgu.shape), _const_spec(wd.shape),
        ],
        out_specs=row(D_MODEL),
        out_shape=jax.ShapeDtypeStruct((t, D_MODEL), F32),
        scratch_shapes=[pltpu.VMEM((tm, FFN_HIDDEN), BF16)],
        compiler_params=pltpu.CompilerParams(
            dimension_semantics=("arbitrary",), vmem_limit_bytes=VMEM_LIMIT_BYTES),
        name="merge_ffn",
    )(x2, ot, ret, gr, gates, wa, wb, wo, gffn, wgu, wd)


def _mla_head_lane_map():
    half = MLA_ROPE // 2
    src = np.full(LANES, MLA_QK, np.int32)
    for i in range(half):
        src[2 * i] = MLA_NOPE + i
        src[RET_QK // 2 + 2 * i] = MLA_NOPE + half + i
    for n in range(MLA_NOPE // 2):
        src[2 * n + 1] = n
        src[MLA_NOPE + n] = MLA_NOPE // 2 + n
    return src


def _take_lanes(w, src, n_valid):
    pad = [(0, 0)] * (w.ndim - 1) + [(0, 1)]
    return jnp.take(jnp.pad(w, pad), np.minimum(src, n_valid), axis=-1)


def _layout_w_in(w_in, src):
    latent = MLA_Q_RANK + MLA_KV_RANK
    rope_src = np.where(src >= MLA_NOPE, src - MLA_NOPE, MLA_ROPE)
    k_rope_blk = _take_lanes(w_in[:, latent:latent + MLA_ROPE], rope_src, MLA_ROPE)
    small = jnp.concatenate([w_in[:, :latent], k_rope_blk], axis=1).astype(BF16)
    return small, w_in.astype(BF16)


def _layout_w_q_b(w_q_b, src):
    w = w_q_b.reshape(MLA_Q_RANK, MLA_HEADS, MLA_QK)
    return _take_lanes(w, src, MLA_QK).reshape(MLA_Q_RANK, MLA_HEADS * LANES).astype(BF16)


def _layout_w_kv_b(w_kv_b, src):
    w = w_kv_b.reshape(MLA_KV_RANK, MLA_HEADS, MLA_NOPE + MLA_V)
    k_nope = _take_lanes(w[:, :, :MLA_NOPE], src, MLA_NOPE)
    v = w[:, :, MLA_NOPE:]
    return jnp.concatenate([k_nope.reshape(MLA_KV_RANK, MLA_HEADS * LANES),
                            v.reshape(MLA_KV_RANK, MLA_HEADS * MLA_V)], axis=1).astype(BF16)


def _rope_freqs():
    half = RET_QK // 2
    inv = ROPE_THETA ** (-(np.arange(LANES) % half) / half)
    return jnp.asarray(inv.reshape(1, LANES), F32)


def kernel(x, positions, g_mix, w_in, g_q_a, w_q_b, g_kv_a, w_kv_b, g_qn, g_kn, w_mla_out,
           ret_decay_fwd, ret_decay_bwd, w_ret_out, w_out, g_ffn, w_gate_up, w_down):
    batch, seq, d = x.shape
    t = batch * seq
    x2 = x.reshape(t, d)
    pos2 = positions.reshape(t, 1).astype(jnp.int32)
    inv = _rope_freqs()

    row = lambda v: v.astype(F32).reshape(1, -1)
    src = _mla_head_lane_map()
    gqn = _take_lanes(row(g_qn), src, MLA_QK) * (MLA_QK ** -0.5 * math.log2(math.e))
    gkn = _take_lanes(row(g_kn), src, MLA_QK)
    gknn = jnp.where(src < MLA_NOPE, gkn, 0.0)
    gknr = jnp.where(src >= MLA_NOPE, gkn, 0.0)

    q, k, vt, qr, kr, vr, gr, gates = _proj_call(
        x2, pos2, row(g_mix), *_layout_w_in(w_in, src), row(g_q_a), _layout_w_q_b(w_q_b, src),
        row(g_kv_a), _layout_w_kv_b(w_kv_b, src), gqn, gknn, gknr, inv)

    ot = _attn_call(q, k, vt, batch, seq)
    dec = jnp.stack([ret_decay_fwd, ret_decay_bwd]).astype(F32)
    ret = _ret_call(dec, qr, kr, vr, batch, seq)

    y = _out_call(x2, ot, ret, gr, gates, w_mla_out.astype(BF16), w_ret_out.astype(BF16),
                  w_out.astype(BF16), row(g_ffn), w_gate_up.astype(BF16), w_down.astype(BF16))
    return y.reshape(batch, seq, d)
```

```python
import functools
import math

import numpy as np
import jax
import jax.numpy as jnp
from jax import lax
from jax.experimental import pallas as pl
from jax.experimental.pallas import tpu as pltpu

F32 = jnp.float32
BF16 = jnp.bfloat16

LANES = 128
VMEM_LIMIT_BYTES = 56 * 1024 * 1024

D_MODEL = 1024
MLA_HEADS = 8
MLA_Q_RANK = 256
MLA_KV_RANK = 128
MLA_NOPE = 64
MLA_ROPE = 32
MLA_V = 64
MLA_QK = MLA_NOPE + MLA_ROPE
RET_HEADS = 8
RET_QK = 64
RET_V = 128
FFN_HIDDEN = 2816
ROPE_THETA = 10000.0
EPS = 1e-6

W_IN_MLA_COLS = MLA_Q_RANK + MLA_KV_RANK + MLA_ROPE
OFF_QK_R = 0
OFF_V_R = OFF_QK_R + 2 * RET_HEADS * RET_QK
OFF_G_R = OFF_V_R + RET_HEADS * RET_V
OFF_GATE = OFF_G_R + RET_HEADS * RET_V
OFF_END = OFF_GATE + 2 * D_MODEL

TM_PROJ = 512
TM_OUT = 512
TQ = 512
TK = 256
S_AHEAD = 2
S_SLOTS = 4
Q_PER_BODY = 4
BF16_SUBLANES = 16
VT_ROWS = MLA_V + BF16_SUBLANES
RET_C = 256
RET_AHEAD = 2
RET_SLOTS = 4
FFN_CHUNK = 256


def _dot(a, b):
    return jnp.dot(a, b, preferred_element_type=F32)


def _rms_scale(v, n):
    return lax.rsqrt(jnp.sum(v * v, axis=-1, keepdims=True) * (1.0 / n) + EPS)


def _const_spec(shape):
    zeros = (0,) * len(shape)
    return pl.BlockSpec(shape, lambda *_: zeros, pipeline_mode=pl.Buffered(1))


def _proj_kernel(x_ref, pos_ref, gmix_ref, wsm_ref, win_ref, gqa_ref, wqb_ref, gkva_ref, wkvb_ref,
                 gqn_ref, gknn_ref, gknr_ref, inv_ref,
                 q_ref, k_ref, vt_ref, qr_ref, kr_ref, vr_ref, gr_ref, gate_ref, wal_ref):
    @pl.when(pl.program_id(0) == 0)
    def _():
        for j in range(OFF_END // LANES):
            c0 = W_IN_MLA_COLS + j * LANES
            wal_ref[:, j * LANES:(j + 1) * LANES] = win_ref[:, c0:c0 + LANES]

    x = x_ref[...]
    h = (x * _rms_scale(x, D_MODEL) * gmix_ref[...]).astype(BF16)

    pos = pos_ref[...].astype(F32)
    lane = lax.broadcasted_iota(jnp.int32, (1, LANES), 1)
    half = RET_QK // 2
    ang = pos * inv_ref[...]
    cs_r = jnp.cos(ang)
    sn = jnp.sin(ang)
    first_half = (lane % RET_QK) < half
    lo_r = jnp.where(first_half, -sn, 0.0)
    hi_r = jnp.where(first_half, 0.0, sn)
    mla_rope = ((lane % 2) == 0) & (lane < RET_QK)
    cs_m = jnp.where(mla_rope, cs_r, 1.0)
    lo_m = jnp.where(mla_rope, lo_r, 0.0)
    hi_m = jnp.where(mla_rope, hi_r, 0.0)

    def rope(v, cs, lo, hi):
        return (v * cs + pltpu.roll(v, LANES - half, 1) * lo + pltpu.roll(v, half, 1) * hi)

    rope_m = functools.partial(rope, cs=cs_m, lo=lo_m, hi=hi_m)
    rope_r = functools.partial(rope, cs=cs_r, lo=lo_r, hi=hi_r)

    small = _dot(h, wsm_ref[...])
    cq = small[:, :MLA_Q_RANK]
    ckv = small[:, MLA_Q_RANK:MLA_Q_RANK + MLA_KV_RANK]
    krb = small[:, MLA_Q_RANK + MLA_KV_RANK:]

    cqn = (cq * _rms_scale(cq, MLA_Q_RANK) * gqa_ref[...]).astype(BF16)
    qf = _dot(cqn, wqb_ref[...])
    gqn = gqn_ref[...]
    gate_ref[:, :D_MODEL] = _dot(h, wal_ref[:, OFF_GATE:OFF_GATE + D_MODEL]).astype(BF16)
    for hd in range(MLA_HEADS):
        blk = qf[:, hd * LANES:(hd + 1) * LANES]
        qn = blk * _rms_scale(blk, MLA_QK) * gqn
        q_ref[:, hd * LANES:(hd + 1) * LANES] = rope_m(qn).astype(BF16)

    ckvn = (ckv * _rms_scale(ckv, MLA_KV_RANK) * gkva_ref[...]).astype(BF16)
    kvf = _dot(ckvn, wkvb_ref[...])
    vr_ref[...] = _dot(h, wal_ref[:, OFF_V_R:OFF_G_R]).astype(BF16)
    kr_roped = rope_m(krb * gknr_ref[...])
    ss_rope = jnp.sum(krb * krb, axis=-1, keepdims=True)
    gknn = gknn_ref[...]
    for hd in range(MLA_HEADS):
        blk = kvf[:, hd * LANES:(hd + 1) * LANES]
        ss = jnp.sum(blk * blk, axis=-1, keepdims=True) + ss_rope
        r = lax.rsqrt(ss * (1.0 / MLA_QK) + EPS)
        k_ref[:, hd * LANES:(hd + 1) * LANES] = ((blk * gknn + kr_roped) * r).astype(BF16)
    gr_ref[...] = _dot(h, wal_ref[:, OFF_G_R:OFF_GATE]).astype(BF16)
    vt = kvf[:, MLA_HEADS * LANES:].T.astype(BF16)
    ones_rows = jnp.where(lax.broadcasted_iota(jnp.int32, (BF16_SUBLANES, vt.shape[1]), 0) == 0,
                          1.0, 0.0).astype(BF16)
    for hd in range(MLA_HEADS):
        vt_ref[hd * VT_ROWS:hd * VT_ROWS + MLA_V, :] = vt[hd * MLA_V:(hd + 1) * MLA_V]
        vt_ref[hd * VT_ROWS + MLA_V:(hd + 1) * VT_ROWS, :] = ones_rows

    qk_r = _dot(h, wal_ref[:, OFF_QK_R:OFF_V_R])
    gate_ref[:, D_MODEL:] = _dot(h, wal_ref[:, OFF_GATE + D_MODEL:OFF_END]).astype(BF16)
    k_off = RET_HEADS * RET_QK
    for j in range(k_off // LANES):
        qr_ref[:, j * LANES:(j + 1) * LANES] = rope_r(qk_r[:, j * LANES:(j + 1) * LANES]).astype(BF16)
        kb = qk_r[:, k_off + j * LANES:k_off + (j + 1) * LANES]
        kr_ref[:, j * LANES:(j + 1) * LANES] = (rope_r(kb) * (RET_QK ** -0.5)).astype(BF16)


def _proj_call(x2, pos2, gmix, wsm, win, gqa, wqb, gkva, wkvb, gqn, gknn, gknr, inv):
    t = x2.shape[0]
    tm = TM_PROJ
    row = lambda w: pl.BlockSpec((tm, w), lambda i: (i, 0))
    out_shape = (
        jax.ShapeDtypeStruct((t, MLA_HEADS * LANES), BF16),
        jax.ShapeDtypeStruct((t, MLA_HEADS * LANES), BF16),
        jax.ShapeDtypeStruct((MLA_HEADS * VT_ROWS, t), BF16),
        jax.ShapeDtypeStruct((t, RET_HEADS * RET_QK), BF16),
        jax.ShapeDtypeStruct((t, RET_HEADS * RET_QK), BF16),
        jax.ShapeDtypeStruct((t, RET_HEADS * RET_V), BF16),
        jax.ShapeDtypeStruct((t, RET_HEADS * RET_V), BF16),
        jax.ShapeDtypeStruct((t, 2 * D_MODEL), BF16),
    )
    return pl.pallas_call(
        _proj_kernel,
        grid=(t // tm,),
        in_specs=[
            row(D_MODEL), row(1),
            _const_spec(gmix.shape), _const_spec(wsm.shape), _const_spec(win.shape),
            _const_spec(gqa.shape),
            _const_spec(wqb.shape), _const_spec(gkva.shape), _const_spec(wkvb.shape),
            _const_spec(gqn.shape), _const_spec(gknn.shape), _const_spec(gknr.shape),
            _const_spec(inv.shape),
        ],
        out_specs=(
            row(MLA_HEADS * LANES), row(MLA_HEADS * LANES),
            pl.BlockSpec((MLA_HEADS * VT_ROWS, tm), lambda i: (0, i)),
            row(RET_HEADS * RET_QK), row(RET_HEADS * RET_QK),
            row(RET_HEADS * RET_V), row(RET_HEADS * RET_V), row(2 * D_MODEL),
        ),
        out_shape=out_shape,
        scratch_shapes=[pltpu.VMEM((D_MODEL, OFF_END), BF16)],
        compiler_params=pltpu.CompilerParams(
            dimension_semantics=("arbitrary",), vmem_limit_bytes=VMEM_LIMIT_BYTES),
        name="proj_in",
    )(x2, pos2, gmix, wsm, win, gqa, wqb, gkva, wkvb, gqn, gknn, gknr, inv)


def _attn_kernel(q_ref, k_ref, vt_ref, *refs, seq, n_w):
    o_ref, s_ref = refs[n_w], refs[-1]
    for w_ref, w_bf_ref in zip(refs[:n_w], refs[n_w + 1:-1]):
        w_bf_ref[...] = w_ref[...].astype(BF16)

    n_q = seq // TQ
    n_k = seq // TK
    assert n_k % S_SLOTS == 0 and n_k > S_AHEAD

    def q_transposed(qi):
        q0 = pl.multiple_of(qi * TQ, TQ)
        return q_ref[pl.ds(q0, TQ), :].astype(F32).T.astype(BF16)

    def scores(qt, ki):
        s = _dot(k_ref[ki * TK:(ki + 1) * TK, :], qt)
        s_ref[ki % S_SLOTS] = s
        return jnp.max(s, axis=0, keepdims=True)

    def q_tile(qi, carry):
        qt, cms = carry
        qt_next = q_transposed(jnp.minimum(qi + 1, n_q - 1))
        m, cms = cms[0], cms[1:]
        acc = jnp.zeros((VT_ROWS, TQ), F32)
        for ki in range(n_k):
            ka = ki + S_AHEAD
            cms = cms + (scores(qt, ka) if ka < n_k else scores(qt_next, ka - n_k),)
            p = jnp.exp2(s_ref[ki % S_SLOTS] - m).astype(BF16)
            acc = acc + _dot(vt_ref[:, ki * TK:(ki + 1) * TK], p)
            if ki + 1 < n_k:
                m_new = jnp.maximum(m, cms[0])
                acc = acc * jnp.exp2(m - m_new)
                m, cms = m_new, cms[1:]
        q0 = pl.multiple_of(qi * TQ, TQ)
        o_ref[:, pl.ds(q0, TQ)] = (acc[:MLA_V] / acc[MLA_V:MLA_V + 1]).astype(BF16)
        return qt_next, cms

    def q_tiles(j, carry):
        for u in range(Q_PER_BODY):
            carry = q_tile(j * Q_PER_BODY + u, carry)
        return carry

    assert n_q % Q_PER_BODY == 0
    qt0 = q_transposed(0)
    lax.fori_loop(0, n_q // Q_PER_BODY, q_tiles,
                  (qt0, tuple(scores(qt0, i) for i in range(S_AHEAD))))


def _attn_call(q, k, vt, weights, batch, seq):
    t = batch * seq
    steps = batch * MLA_HEADS
    slabs = []
    for w in weights:
        rows = w.shape[0] // steps
        assert rows * steps == w.shape[0] and rows % BF16_SUBLANES == 0
        slabs.append(pl.BlockSpec((rows, w.shape[1]), lambda b, h: (b * MLA_HEADS + h, 0)))
    out = pl.pallas_call(
        functools.partial(_attn_kernel, seq=seq, n_w=len(weights)),
        grid=(batch, MLA_HEADS),
        in_specs=[
            pl.BlockSpec((seq, LANES), lambda b, h: (b, h)),
            pl.BlockSpec((seq, LANES), lambda b, h: (b, h)),
            pl.BlockSpec((VT_ROWS, seq), lambda b, h: (h, b)),
            *slabs,
        ],
        out_specs=(pl.BlockSpec((MLA_V, seq), lambda b, h: (h, b)), *slabs),
        out_shape=(jax.ShapeDtypeStruct((MLA_HEADS * MLA_V, t), BF16),
                   *(jax.ShapeDtypeStruct(w.shape, BF16) for w in weights)),
        scratch_shapes=[pltpu.VMEM((S_SLOTS, TK, TQ), F32)],
        compiler_params=pltpu.CompilerParams(
            dimension_semantics=("arbitrary", "arbitrary"), vmem_limit_bytes=VMEM_LIMIT_BYTES),
        name="mla_attention",
    )(q, k, vt, *weights)
    return out[0], out[1:]


def _ret_kernel(dec_ref, q_ref, k_ref, v_ref, o_ref, kv_ref, st_ref, a_ref, *, seq):
    c = RET_C
    n = seq // c
    hd = pl.program_id(1)
    lo = (hd % 2) * RET_QK
    lane = lax.broadcasted_iota(jnp.int32, (1, LANES), 1)
    own = (lane >= lo) & (lane < lo + RET_QK)
    row = lax.broadcasted_iota(jnp.int32, (LANES, 1), 0)
    own_row = (row >= lo) & (row < lo + RET_QK)
    own_bf = jnp.where(own, 1.0, 0.0).astype(BF16)

    lg_f = -jnp.exp(jnp.full((1, 1), dec_ref[0, hd], F32))
    lg_b = -jnp.exp(jnp.full((1, 1), dec_ref[1, hd], F32))
    a = lax.broadcasted_iota(jnp.int32, (c, 1), 0).astype(F32)
    b = lax.broadcasted_iota(jnp.int32, (1, c), 1).astype(F32)
    q_decay = jnp.where(own, jnp.exp(lg_f * (a + 1.0)), jnp.exp(lg_b * (c - a)))
    k_decay = jnp.where(own, jnp.exp(lg_f * (c - 1.0 - a)), jnp.exp(lg_b * a))
    cd_f = jnp.exp(lg_f * float(c))
    cd_b = jnp.exp(lg_b * float(c))
    diff = a - b
    decay = jnp.where(diff >= 0, jnp.exp(lg_f * jnp.maximum(diff, 0.0)),
                      jnp.exp(lg_b * jnp.maximum(-diff, 0.0)))

    def both_halves(v):
        return jnp.where(own, v, pltpu.roll(v, RET_QK, 1))

    def chunk_kv(i, carry):
        r0 = pl.multiple_of(i * c, c)
        k = k_ref[pl.ds(r0, c), :].astype(F32)
        kv_ref[i] = _dot((both_halves(k) * k_decay).T.astype(BF16), v_ref[pl.ds(r0, c), :])
        return carry

    lax.fori_loop(0, n, chunk_kv, 0, unroll=True)

    def fwd_scan(i, f):
        st_ref[i] = f.astype(BF16)
        return f * cd_f + kv_ref[i]

    lax.fori_loop(0, n, fwd_scan, jnp.zeros((LANES, RET_V), F32))

    def bwd_scan(j, s):
        i = n - 1 - j
        st_ref[i] = jnp.where(own_row, st_ref[i].astype(F32), s).astype(BF16)
        return s * cd_b + kv_ref[i]

    lax.fori_loop(0, n, bwd_scan, jnp.zeros((LANES, RET_V), F32))

    def scores(i):
        rows = slice(i * c, (i + 1) * c)
        a_ref[i % RET_SLOTS] = lax.dot_general(
            q_ref[rows, :] * own_bf, k_ref[rows, :], (((1,), (1,)), ((), ())),
            preferred_element_type=F32)

    for i in range(min(RET_AHEAD, n)):
        scores(i)
    for i in range(n):
        if i + RET_AHEAD < n:
            scores(i + RET_AHEAD)
        rows = slice(i * c, (i + 1) * c)
        p = (a_ref[i % RET_SLOTS] * decay).astype(BF16)
        qd = (both_halves(q_ref[rows, :].astype(F32)) * q_decay).astype(BF16)
        o = _dot(jnp.concatenate([p, qd], axis=1),
                 jnp.concatenate([v_ref[rows, :], st_ref[i]], axis=0))
        o_ref[rows, :] = (o * _rms_scale(o, RET_V)).astype(BF16)


def _ret_call(dec, qr, kr, vr, batch, seq):
    t = batch * seq
    pair = pl.BlockSpec((seq, LANES), lambda b, h: (b, h // 2))
    head = pl.BlockSpec((seq, LANES), lambda b, h: (b, h))
    return pl.pallas_call(
        functools.partial(_ret_kernel, seq=seq),
        grid=(batch, RET_HEADS),
        in_specs=[pl.BlockSpec(memory_space=pltpu.SMEM), pair, pair, head],
        out_specs=head,
        out_shape=jax.ShapeDtypeStruct((t, RET_HEADS * RET_V), BF16),
        scratch_shapes=[pltpu.VMEM((seq // RET_C, LANES, RET_V), F32),
                        pltpu.VMEM((seq // RET_C, LANES, RET_V), BF16),
                        pltpu.VMEM((RET_SLOTS, RET_C, RET_C), F32)],
        compiler_params=pltpu.CompilerParams(
            dimension_semantics=("arbitrary", "arbitrary"), vmem_limit_bytes=VMEM_LIMIT_BYTES),
        name="retention",
    )(dec, qr, kr, vr)


def _out_kernel(x_ref, ot_ref, ret_ref, gr_ref, gate_ref, wa_ref, wb_ref, wo_ref, gffn_ref,
                wgu_ref, wd_ref, y_ref, act_ref):
    o_a = ot_ref[...].astype(F32).T.astype(BF16)
    y_a = _dot(o_a, wa_ref[...])
    g = gr_ref[...].astype(F32)
    o_b = (g * jax.nn.sigmoid(g) * ret_ref[...].astype(F32)).astype(BF16)
    y_b = _dot(o_b, wb_ref[...])
    merged = (jax.nn.sigmoid(gate_ref[:, :D_MODEL].astype(F32)) * y_a
              + jax.nn.sigmoid(gate_ref[:, D_MODEL:].astype(F32)) * y_b).astype(BF16)
    x1 = x_ref[...] + _dot(merged, wo_ref[...])
    h2 = (x1 * _rms_scale(x1, D_MODEL) * gffn_ref[...]).astype(BF16)
    for cidx in range(FFN_HIDDEN // FFN_CHUNK):
        c0 = cidx * FFN_CHUNK
        gate = _dot(h2, wgu_ref[:, c0:c0 + FFN_CHUNK])
        up = _dot(h2, wgu_ref[:, FFN_HIDDEN + c0:FFN_HIDDEN + c0 + FFN_CHUNK])
        act_ref[:, c0:c0 + FFN_CHUNK] = (gate * jax.nn.sigmoid(gate) * up).astype(BF16)
    y_ref[...] = x1 + _dot(act_ref[...], wd_ref[...])


def _out_call(x2, ot, ret, gr, gates, wa, wb, wo, gffn, wgu, wd):
    t = x2.shape[0]
    tm = TM_OUT
    row = lambda w: pl.BlockSpec((tm, w), lambda i: (i, 0))
    return pl.pallas_call(
        _out_kernel,
        grid=(t // tm,),
        in_specs=[
            row(D_MODEL),
            pl.BlockSpec((MLA_HEADS * MLA_V, tm), lambda i: (0, i)),
            row(RET_HEADS * RET_V), row(RET_HEADS * RET_V), row(2 * D_MODEL),
            _const_spec(wa.shape), _const_spec(wb.shape), _const_spec(wo.shape),
            _const_spec(gffn.shape), _const_spec(wgu.shape), _const_spec(wd.shape),
        ],
        out_specs=row(D_MODEL),
        out_shape=jax.ShapeDtypeStruct((t, D_MODEL), F32),
        scratch_shapes=[pltpu.VMEM((tm, FFN_HIDDEN), BF16)],
        compiler_params=pltpu.CompilerParams(
            dimension_semantics=("arbitrary",), vmem_limit_bytes=VMEM_LIMIT_BYTES),
        name="merge_ffn",
    )(x2, ot, ret, gr, gates, wa, wb, wo, gffn, wgu, wd)


def _mla_head_lane_map():
    half = MLA_ROPE // 2
    src = np.full(LANES, MLA_QK, np.int32)
    for i in range(half):
        src[2 * i] = MLA_NOPE + i
        src[RET_QK // 2 + 2 * i] = MLA_NOPE + half + i
    for n in range(MLA_NOPE // 2):
        src[2 * n + 1] = n
        src[MLA_NOPE + n] = MLA_NOPE // 2 + n
    return src


def _take_lanes(w, src, n_valid):
    pad = [(0, 0)] * (w.ndim - 1) + [(0, 1)]
    return jnp.take(jnp.pad(w, pad), np.minimum(src, n_valid), axis=-1)


def _layout_w_in(w_in, src):
    latent = MLA_Q_RANK + MLA_KV_RANK
    rope_src = np.where(src >= MLA_NOPE, src - MLA_NOPE, MLA_ROPE)
    k_rope_blk = _take_lanes(w_in[:, latent:latent + MLA_ROPE], rope_src, MLA_ROPE)
    small = jnp.concatenate([w_in[:, :latent], k_rope_blk], axis=1).astype(BF16)
    return small, w_in.astype(BF16)


def _layout_w_q_b(w_q_b, src):
    w = w_q_b.reshape(MLA_Q_RANK, MLA_HEADS, MLA_QK)
    return _take_lanes(w, src, MLA_QK).reshape(MLA_Q_RANK, MLA_HEADS * LANES).astype(BF16)


def _layout_w_kv_b(w_kv_b, src):
    w = w_kv_b.reshape(MLA_KV_RANK, MLA_HEADS, MLA_NOPE + MLA_V)
    k_nope = _take_lanes(w[:, :, :MLA_NOPE], src, MLA_NOPE)
    v = w[:, :, MLA_NOPE:]
    return jnp.concatenate([k_nope.reshape(MLA_KV_RANK, MLA_HEADS * LANES),
                            v.reshape(MLA_KV_RANK, MLA_HEADS * MLA_V)], axis=1).astype(BF16)


def _rope_freqs():
    half = RET_QK // 2
    inv = ROPE_THETA ** (-(np.arange(LANES) % half) / half)
    return jnp.asarray(inv.reshape(1, LANES), F32)


def kernel(x, positions, g_mix, w_in, g_q_a, w_q_b, g_kv_a, w_kv_b, g_qn, g_kn, w_mla_out,
           ret_decay_fwd, ret_decay_bwd, w_ret_out, w_out, g_ffn, w_gate_up, w_down):
    batch, seq, d = x.shape
    t = batch * seq
    x2 = x.reshape(t, d)
    pos2 = positions.reshape(t, 1).astype(jnp.int32)
    inv = _rope_freqs()

    row = lambda v: v.astype(F32).reshape(1, -1)
    src = _mla_head_lane_map()
    gqn = _take_lanes(row(g_qn), src, MLA_QK) * (MLA_QK ** -0.5 * math.log2(math.e))
    gkn = _take_lanes(row(g_kn), src, MLA_QK)
    gknn = jnp.where(src < MLA_NOPE, gkn, 0.0)
    gknr = jnp.where(src >= MLA_NOPE, gkn, 0.0)

    q, k, vt, qr, kr, vr, gr, gates = _proj_call(
        x2, pos2, row(g_mix), *_layout_w_in(w_in, src), row(g_q_a), _layout_w_q_b(w_q_b, src),
        row(g_kv_a), _layout_w_kv_b(w_kv_b, src), gqn, gknn, gknr, inv)

    ot, (wa, wb, wo, wgu, wd) = _attn_call(
        q, k, vt, (w_mla_out, w_ret_out, w_out, w_gate_up, w_down), batch, seq)
    dec = jnp.stack([ret_decay_fwd, ret_decay_bwd]).astype(F32)
    ret = _ret_call(dec, qr, kr, vr, batch, seq)

    y = _out_call(x2, ot, ret, gr, gates, wa, wb, wo, row(g_ffn), wgu, wd)
    return y.reshape(batch, seq, d)
```

```python
import functools
import math

import numpy as np
import jax
import jax.numpy as jnp
from jax import lax
from jax.experimental import pallas as pl
from jax.experimental.pallas import tpu as pltpu

F32 = jnp.float32
BF16 = jnp.bfloat16

LANES = 128
VMEM_LIMIT_BYTES = 56 * 1024 * 1024

D_MODEL = 1024
MLA_HEADS = 8
MLA_Q_RANK = 256
MLA_KV_RANK = 128
MLA_NOPE = 64
MLA_ROPE = 32
MLA_V = 64
MLA_QK = MLA_NOPE + MLA_ROPE
RET_HEADS = 8
RET_QK = 64
RET_V = 128
FFN_HIDDEN = 2816
ROPE_THETA = 10000.0
EPS = 1e-6

W_IN_MLA_COLS = MLA_Q_RANK + MLA_KV_RANK + MLA_ROPE
W_IN_PREP_COLS = 512
OFF_QK_R = 0
OFF_V_R = OFF_QK_R + 2 * RET_HEADS * RET_QK
OFF_G_R = OFF_V_R + RET_HEADS * RET_V
OFF_GATE = OFF_G_R + RET_HEADS * RET_V
OFF_END = OFF_GATE + 2 * D_MODEL

TM_PROJ = 512
TM_OUT = 512
TQ = 512
TK = 256
S_AHEAD = 2
S_SLOTS = 4
Q_PER_BODY = 4
BF16_SUBLANES = 16
VT_ROWS = MLA_V + BF16_SUBLANES
RET_C = 256
RET_AHEAD = 2
RET_SLOTS = 4
FFN_CHUNK = 256


def _dot(a, b):
    return jnp.dot(a, b, preferred_element_type=F32)


def _rms_scale(v, n):
    return lax.rsqrt(jnp.sum(v * v, axis=-1, keepdims=True) * (1.0 / n) + EPS)


def _const_spec(shape):
    zeros = (0,) * len(shape)
    return pl.BlockSpec(shape, lambda *_: zeros, pipeline_mode=pl.Buffered(1))


def _w_in_rest_kernel(w_ref, o_ref):
    shift = W_IN_MLA_COLS % LANES
    o_ref[...] = w_ref[:, shift:shift + W_IN_PREP_COLS].astype(BF16)


def _w_in_rest_call(w_in):
    rows = w_in.shape[0]
    start = W_IN_MLA_COLS - W_IN_MLA_COLS % LANES
    return pl.pallas_call(
        _w_in_rest_kernel,
        grid=(OFF_END // W_IN_PREP_COLS,),
        in_specs=[pl.BlockSpec((pl.Element(rows), pl.Element(W_IN_PREP_COLS + LANES)),
                               lambda j: (0, pl.multiple_of(start + j * W_IN_PREP_COLS, LANES)))],
        out_specs=pl.BlockSpec((rows, W_IN_PREP_COLS), lambda j: (0, j)),
        out_shape=jax.ShapeDtypeStruct((rows, OFF_END), BF16),
        compiler_params=pltpu.CompilerParams(dimension_semantics=("arbitrary",)),
        name="w_in_rest",
    )(w_in)


def _proj_kernel(x_ref, pos_ref, gmix_ref, wsm_ref, wal_ref, gqa_ref, wqb_ref, gkva_ref, wkvb_ref,
                 gqn_ref, gknn_ref, gknr_ref, inv_ref,
                 q_ref, k_ref, vt_ref, qr_ref, kr_ref, vr_ref, gr_ref, gate_ref):
    x = x_ref[...]
    h = (x * _rms_scale(x, D_MODEL) * gmix_ref[...]).astype(BF16)

    pos = pos_ref[...].astype(F32)
    lane = lax.broadcasted_iota(jnp.int32, (1, LANES), 1)
    half = RET_QK // 2
    ang = pos * inv_ref[...]
    cs_r = jnp.cos(ang)
    sn = jnp.sin(ang)
    first_half = (lane % RET_QK) < half
    lo_r = jnp.where(first_half, -sn, 0.0)
    hi_r = jnp.where(first_half, 0.0, sn)
    mla_rope = ((lane % 2) == 0) & (lane < RET_QK)
    cs_m = jnp.where(mla_rope, cs_r, 1.0)
    lo_m = jnp.where(mla_rope, lo_r, 0.0)
    hi_m = jnp.where(mla_rope, hi_r, 0.0)

    def rope(v, cs, lo, hi):
        return (v * cs + pltpu.roll(v, LANES - half, 1) * lo + pltpu.roll(v, half, 1) * hi)

    rope_m = functools.partial(rope, cs=cs_m, lo=lo_m, hi=hi_m)
    rope_r = functools.partial(rope, cs=cs_r, lo=lo_r, hi=hi_r)

    small = _dot(h, wsm_ref[...])
    cq = small[:, :MLA_Q_RANK]
    ckv = small[:, MLA_Q_RANK:MLA_Q_RANK + MLA_KV_RANK]
    krb = small[:, MLA_Q_RANK + MLA_KV_RANK:]

    cqn = (cq * _rms_scale(cq, MLA_Q_RANK) * gqa_ref[...]).astype(BF16)
    qf = _dot(cqn, wqb_ref[...])
    gqn = gqn_ref[...]
    gate_ref[:, :D_MODEL] = _dot(h, wal_ref[:, OFF_GATE:OFF_GATE + D_MODEL]).astype(BF16)
    for hd in range(MLA_HEADS):
        blk = qf[:, hd * LANES:(hd + 1) * LANES]
        qn = blk * _rms_scale(blk, MLA_QK) * gqn
        q_ref[:, hd * LANES:(hd + 1) * LANES] = rope_m(qn).astype(BF16)

    ckvn = (ckv * _rms_scale(ckv, MLA_KV_RANK) * gkva_ref[...]).astype(BF16)
    kvf = _dot(ckvn, wkvb_ref[...])
    vr_ref[...] = _dot(h, wal_ref[:, OFF_V_R:OFF_G_R]).astype(BF16)
    kr_roped = rope_m(krb * gknr_ref[...])
    ss_rope = jnp.sum(krb * krb, axis=-1, keepdims=True)
    gknn = gknn_ref[...]
    for hd in range(MLA_HEADS):
        blk = kvf[:, hd * LANES:(hd + 1) * LANES]
        ss = jnp.sum(blk * blk, axis=-1, keepdims=True) + ss_rope
        r = lax.rsqrt(ss * (1.0 / MLA_QK) + EPS)
        k_ref[:, hd * LANES:(hd + 1) * LANES] = ((blk * gknn + kr_roped) * r).astype(BF16)
    gr_ref[...] = _dot(h, wal_ref[:, OFF_G_R:OFF_GATE]).astype(BF16)
    vt = kvf[:, MLA_HEADS * LANES:].T.astype(BF16)
    ones_rows = jnp.where(lax.broadcasted_iota(jnp.int32, (BF16_SUBLANES, vt.shape[1]), 0) == 0,
                          1.0, 0.0).astype(BF16)
    for hd in range(MLA_HEADS):
        vt_ref[hd * VT_ROWS:hd * VT_ROWS + MLA_V, :] = vt[hd * MLA_V:(hd + 1) * MLA_V]
        vt_ref[hd * VT_ROWS + MLA_V:(hd + 1) * VT_ROWS, :] = ones_rows

    qk_r = _dot(h, wal_ref[:, OFF_QK_R:OFF_V_R])
    gate_ref[:, D_MODEL:] = _dot(h, wal_ref[:, OFF_GATE + D_MODEL:OFF_END]).astype(BF16)
    k_off = RET_HEADS * RET_QK
    for j in range(k_off // LANES):
        qr_ref[:, j * LANES:(j + 1) * LANES] = rope_r(qk_r[:, j * LANES:(j + 1) * LANES]).astype(BF16)
        kb = qk_r[:, k_off + j * LANES:k_off + (j + 1) * LANES]
        kr_ref[:, j * LANES:(j + 1) * LANES] = (rope_r(kb) * (RET_QK ** -0.5)).astype(BF16)


def _proj_call(x2, pos2, gmix, wsm, win, gqa, wqb, gkva, wkvb, gqn, gknn, gknr, inv):
    t = x2.shape[0]
    tm = TM_PROJ
    row = lambda w: pl.BlockSpec((tm, w), lambda i: (i, 0))
    out_shape = (
        jax.ShapeDtypeStruct((t, MLA_HEADS * LANES), BF16),
        jax.ShapeDtypeStruct((t, MLA_HEADS * LANES), BF16),
        jax.ShapeDtypeStruct((MLA_HEADS * VT_ROWS, t), BF16),
        jax.ShapeDtypeStruct((t, RET_HEADS * RET_QK), BF16),
        jax.ShapeDtypeStruct((t, RET_HEADS * RET_QK), BF16),
        jax.ShapeDtypeStruct((t, RET_HEADS * RET_V), BF16),
        jax.ShapeDtypeStruct((t, RET_HEADS * RET_V), BF16),
        jax.ShapeDtypeStruct((t, 2 * D_MODEL), BF16),
    )
    return pl.pallas_call(
        _proj_kernel,
        grid=(t // tm,),
        in_specs=[
            row(D_MODEL), row(1),
            _const_spec(gmix.shape), _const_spec(wsm.shape), _const_spec(win.shape),
            _const_spec(gqa.shape),
            _const_spec(wqb.shape), _const_spec(gkva.shape), _const_spec(wkvb.shape),
            _const_spec(gqn.shape), _const_spec(gknn.shape), _const_spec(gknr.shape),
            _const_spec(inv.shape),
        ],
        out_specs=(
            row(MLA_HEADS * LANES), row(MLA_HEADS * LANES),
            pl.BlockSpec((MLA_HEADS * VT_ROWS, tm), lambda i: (0, i)),
            row(RET_HEADS * RET_QK), row(RET_HEADS * RET_QK),
            row(RET_HEADS * RET_V), row(RET_HEADS * RET_V), row(2 * D_MODEL),
        ),
        out_shape=out_shape,
        compiler_params=pltpu.CompilerParams(
            dimension_semantics=("arbitrary",), vmem_limit_bytes=VMEM_LIMIT_BYTES),
        name="proj_in",
    )(x2, pos2, gmix, wsm, win, gqa, wqb, gkva, wkvb, gqn, gknn, gknr, inv)


def _attn_kernel(q_ref, k_ref, vt_ref, *refs, seq, n_w):
    o_ref, s_ref = refs[n_w], refs[-1]
    for w_ref, w_bf_ref in zip(refs[:n_w], refs[n_w + 1:-1]):
        w_bf_ref[...] = w_ref[...].astype(BF16)

    n_q = seq // TQ
    n_k = seq // TK
    assert n_k % S_SLOTS == 0 and n_k > S_AHEAD

    def q_transposed(qi):
        q0 = pl.multiple_of(qi * TQ, TQ)
        return q_ref[pl.ds(q0, TQ), :].astype(F32).T.astype(BF16)

    def scores(qt, ki):
        s = _dot(k_ref[ki * TK:(ki + 1) * TK, :], qt)
        s_ref[ki % S_SLOTS] = s
        return jnp.max(s, axis=0, keepdims=True)

    def q_tile(qi, carry):
        qt, cms = carry
        qt_next = q_transposed(jnp.minimum(qi + 1, n_q - 1))
        m, cms = cms[0], cms[1:]
        acc = jnp.zeros((VT_ROWS, TQ), F32)
        for ki in range(n_k):
            ka = ki + S_AHEAD
            cms = cms + (scores(qt, ka) if ka < n_k else scores(qt_next, ka - n_k),)
            p = jnp.exp2(s_ref[ki % S_SLOTS] - m).astype(BF16)
            acc = acc + _dot(vt_ref[:, ki * TK:(ki + 1) * TK], p)
            if ki + 1 < n_k:
                m_new = jnp.maximum(m, cms[0])
                acc = acc * jnp.exp2(m - m_new)
                m, cms = m_new, cms[1:]
        q0 = pl.multiple_of(qi * TQ, TQ)
        o_ref[:, pl.ds(q0, TQ)] = (acc[:MLA_V] / acc[MLA_V:MLA_V + 1]).astype(BF16)
        return qt_next, cms

    def q_tiles(j, carry):
        for u in range(Q_PER_BODY):
            carry = q_tile(j * Q_PER_BODY + u, carry)
        return carry

    assert n_q % Q_PER_BODY == 0
    qt0 = q_transposed(0)
    lax.fori_loop(0, n_q // Q_PER_BODY, q_tiles,
                  (qt0, tuple(scores(qt0, i) for i in range(S_AHEAD))))


def _attn_call(q, k, vt, weights, batch, seq):
    t = batch * seq
    steps = batch * MLA_HEADS
    slabs = []
    for w in weights:
        rows = w.shape[0] // steps
        assert rows * steps == w.shape[0] and rows % BF16_SUBLANES == 0
        slabs.append(pl.BlockSpec((rows, w.shape[1]), lambda b, h: (b * MLA_HEADS + h, 0)))
    out = pl.pallas_call(
        functools.partial(_attn_kernel, seq=seq, n_w=len(weights)),
        grid=(batch, MLA_HEADS),
        in_specs=[
            pl.BlockSpec((seq, LANES), lambda b, h: (b, h)),
            pl.BlockSpec((seq, LANES), lambda b, h: (b, h)),
            pl.BlockSpec((VT_ROWS, seq), lambda b, h: (h, b)),
            *slabs,
        ],
        out_specs=(pl.BlockSpec((MLA_V, seq), lambda b, h: (h, b)), *slabs),
        out_shape=(jax.ShapeDtypeStruct((MLA_HEADS * MLA_V, t), BF16),
                   *(jax.ShapeDtypeStruct(w.shape, BF16) for w in weights)),
        scratch_shapes=[pltpu.VMEM((S_SLOTS, TK, TQ), F32)],
        compiler_params=pltpu.CompilerParams(
            dimension_semantics=("arbitrary", "arbitrary"), vmem_limit_bytes=VMEM_LIMIT_BYTES),
        name="mla_attention",
    )(q, k, vt, *weights)
    return out[0], out[1:]


def _ret_kernel(dec_ref, q_ref, k_ref, v_ref, o_ref, kv_ref, st_ref, a_ref, *, seq):
    c = RET_C
    n = seq // c
    hd = pl.program_id(1)
    lo = (hd % 2) * RET_QK
    lane = lax.broadcasted_iota(jnp.int32, (1, LANES), 1)
    own = (lane >= lo) & (lane < lo + RET_QK)
    row = lax.broadcasted_iota(jnp.int32, (LANES, 1), 0)
    own_row = (row >= lo) & (row < lo + RET_QK)
    own_bf = jnp.where(own, 1.0, 0.0).astype(BF16)

    lg_f = -jnp.exp(jnp.full((1, 1), dec_ref[0, hd], F32))
    lg_b = -jnp.exp(jnp.full((1, 1), dec_ref[1, hd], F32))
    a = lax.broadcasted_iota(jnp.int32, (c, 1), 0).astype(F32)
    b = lax.broadcasted_iota(jnp.int32, (1, c), 1).astype(F32)
    q_decay = jnp.where(own, jnp.exp(lg_f * (a + 1.0)), jnp.exp(lg_b * (c - a)))
    k_decay = jnp.where(own, jnp.exp(lg_f * (c - 1.0 - a)), jnp.exp(lg_b * a))
    cd_f = jnp.exp(lg_f * float(c))
    cd_b = jnp.exp(lg_b * float(c))
    diff = a - b
    decay = jnp.where(diff >= 0, jnp.exp(lg_f * jnp.maximum(diff, 0.0)),
                      jnp.exp(lg_b * jnp.maximum(-diff, 0.0)))

    def both_halves(v):
        return jnp.where(own, v, pltpu.roll(v, RET_QK, 1))

    def chunk_kv(i, carry):
        r0 = pl.multiple_of(i * c, c)
        k = k_ref[pl.ds(r0, c), :].astype(F32)
        kv_ref[i] = _dot((both_halves(k) * k_decay).T.astype(BF16), v_ref[pl.ds(r0, c), :])
        return carry

    lax.fori_loop(0, n, chunk_kv, 0, unroll=True)

    def fwd_scan(i, f):
        st_ref[i] = f.astype(BF16)
        return f * cd_f + kv_ref[i]

    lax.fori_loop(0, n, fwd_scan, jnp.zeros((LANES, RET_V), F32))

    def bwd_scan(j, s):
        i = n - 1 - j
        st_ref[i] = jnp.where(own_row, st_ref[i].astype(F32), s).astype(BF16)
        return s * cd_b + kv_ref[i]

    lax.fori_loop(0, n, bwd_scan, jnp.zeros((LANES, RET_V), F32))

    def scores(i):
        rows = slice(i * c, (i + 1) * c)
        a_ref[i % RET_SLOTS] = lax.dot_general(
            q_ref[rows, :] * own_bf, k_ref[rows, :], (((1,), (1,)), ((), ())),
            preferred_element_type=F32)

    for i in range(min(RET_AHEAD, n)):
        scores(i)
    for i in range(n):
        if i + RET_AHEAD < n:
            scores(i + RET_AHEAD)
        rows = slice(i * c, (i + 1) * c)
        p = (a_ref[i % RET_SLOTS] * decay).astype(BF16)
        qd = (both_halves(q_ref[rows, :].astype(F32)) * q_decay).astype(BF16)
        o = _dot(jnp.concatenate([p, qd], axis=1),
                 jnp.concatenate([v_ref[rows, :], st_ref[i]], axis=0))
        o_ref[rows, :] = (o * _rms_scale(o, RET_V)).astype(BF16)


def _ret_call(dec, qr, kr, vr, batch, seq):
    t = batch * seq
    pair = pl.BlockSpec((seq, LANES), lambda b, h: (b, h // 2))
    head = pl.BlockSpec((seq, LANES), lambda b, h: (b, h))
    return pl.pallas_call(
        functools.partial(_ret_kernel, seq=seq),
        grid=(batch, RET_HEADS),
        in_specs=[pl.BlockSpec(memory_space=pltpu.SMEM), pair, pair, head],
        out_specs=head,
        out_shape=jax.ShapeDtypeStruct((t, RET_HEADS * RET_V), BF16),
        scratch_shapes=[pltpu.VMEM((seq // RET_C, LANES, RET_V), F32),
                        pltpu.VMEM((seq // RET_C, LANES, RET_V), BF16),
                        pltpu.VMEM((RET_SLOTS, RET_C, RET_C), F32)],
        compiler_params=pltpu.CompilerParams(
            dimension_semantics=("arbitrary", "arbitrary"), vmem_limit_bytes=VMEM_LIMIT_BYTES),
        name="retention",
    )(dec, qr, kr, vr)


def _out_kernel(x_ref, ot_ref, ret_ref, gr_ref, gate_ref, wa_ref, wb_ref, wo_ref, gffn_ref,
                wgu_ref, wd_ref, y_ref, act_ref):
    o_a = ot_ref[...].astype(F32).T.astype(BF16)
    y_a = _dot(o_a, wa_ref[...])
    g = gr_ref[...].astype(F32)
    o_b = (g * jax.nn.sigmoid(g) * ret_ref[...].astype(F32)).astype(BF16)
    y_b = _dot(o_b, wb_ref[...])
    merged = (jax.nn.sigmoid(gate_ref[:, :D_MODEL].astype(F32)) * y_a
              + jax.nn.sigmoid(gate_ref[:, D_MODEL:].astype(F32)) * y_b).astype(BF16)
    x1 = x_ref[...] + _dot(merged, wo_ref[...])
    h2 = (x1 * _rms_scale(x1, D_MODEL) * gffn_ref[...]).astype(BF16)
    for cidx in range(FFN_HIDDEN // FFN_CHUNK):
        c0 = cidx * FFN_CHUNK
        gate = _dot(h2, wgu_ref[:, c0:c0 + FFN_CHUNK])
        up = _dot(h2, wgu_ref[:, FFN_HIDDEN + c0:FFN_HIDDEN + c0 + FFN_CHUNK])
        act_ref[:, c0:c0 + FFN_CHUNK] = (gate * jax.nn.sigmoid(gate) * up).astype(BF16)
    y_ref[...] = x1 + _dot(act_ref[...], wd_ref[...])


def _out_call(x2, ot, ret, gr, gates, wa, wb, wo, gffn, wgu, wd):
    t = x2.shape[0]
    tm = TM_OUT
    row = lambda w: pl.BlockSpec((tm, w), lambda i: (i, 0))
    return pl.pallas_call(
        _out_kernel,
        grid=(t // tm,),
        in_specs=[
            row(D_MODEL),
            pl.BlockSpec((MLA_HEADS * MLA_V, tm), lambda i: (0, i)),
            row(RET_HEADS * RET_V), row(RET_HEADS * RET_V), row(2 * D_MODEL),
            _const_spec(wa.shape), _const_spec(wb.shape), _const_spec(wo.shape),
            _const_spec(gffn.shape), _const_spec(wgu.shape), _const_spec(wd.shape),
        ],
        out_specs=row(D_MODEL),
        out_shape=jax.ShapeDtypeStruct((t, D_MODEL), F32),
        scratch_shapes=[pltpu.VMEM((tm, FFN_HIDDEN), BF16)],
        compiler_params=pltpu.CompilerParams(
            dimension_semantics=("arbitrary",), vmem_limit_bytes=VMEM_LIMIT_BYTES),
        name="merge_ffn",
    )(x2, ot, ret, gr, gates, wa, wb, wo, gffn, wgu, wd)


def _mla_head_lane_map():
    half = MLA_ROPE // 2
    src = np.full(LANES, MLA_QK, np.int32)
    for i in range(half):
        src[2 * i] = MLA_NOPE + i
        src[RET_QK // 2 + 2 * i] = MLA_NOPE + half + i
    for n in range(MLA_NOPE // 2):
        src[2 * n + 1] = n
        src[MLA_NOPE + n] = MLA_NOPE // 2 + n
    return src


def _take_lanes(w, src, n_valid):
    pad = [(0, 0)] * (w.ndim - 1) + [(0, 1)]
    return jnp.take(jnp.pad(w, pad), np.minimum(src, n_valid), axis=-1)


def _layout_w_in_mla(w_in, src):
    latent = MLA_Q_RANK + MLA_KV_RANK
    rope_src = np.where(src >= MLA_NOPE, src - MLA_NOPE, MLA_ROPE)
    k_rope_blk = _take_lanes(w_in[:, latent:latent + MLA_ROPE], rope_src, MLA_ROPE)
    return jnp.concatenate([w_in[:, :latent], k_rope_blk], axis=1).astype(BF16)


def _layout_w_q_b(w_q_b, src):
    w = w_q_b.reshape(MLA_Q_RANK, MLA_HEADS, MLA_QK)
    return _take_lanes(w, src, MLA_QK).reshape(MLA_Q_RANK, MLA_HEADS * LANES).astype(BF16)


def _layout_w_kv_b(w_kv_b, src):
    w = w_kv_b.reshape(MLA_KV_RANK, MLA_HEADS, MLA_NOPE + MLA_V)
    k_nope = _take_lanes(w[:, :, :MLA_NOPE], src, MLA_NOPE)
    v = w[:, :, MLA_NOPE:]
    return jnp.concatenate([k_nope.reshape(MLA_KV_RANK, MLA_HEADS * LANES),
                            v.reshape(MLA_KV_RANK, MLA_HEADS * MLA_V)], axis=1).astype(BF16)


def _rope_freqs():
    half = RET_QK // 2
    inv = ROPE_THETA ** (-(np.arange(LANES) % half) / half)
    return jnp.asarray(inv.reshape(1, LANES), F32)


def kernel(x, positions, g_mix, w_in, g_q_a, w_q_b, g_kv_a, w_kv_b, g_qn, g_kn, w_mla_out,
           ret_decay_fwd, ret_decay_bwd, w_ret_out, w_out, g_ffn, w_gate_up, w_down):
    batch, seq, d = x.shape
    t = batch * seq
    x2 = x.reshape(t, d)
    pos2 = positions.reshape(t, 1).astype(jnp.int32)
    inv = _rope_freqs()

    row = lambda v: v.astype(F32).reshape(1, -1)
    src = _mla_head_lane_map()
    gqn = _take_lanes(row(g_qn), src, MLA_QK) * (MLA_QK ** -0.5 * math.log2(math.e))
    gkn = _take_lanes(row(g_kn), src, MLA_QK)
    gknn = jnp.where(src < MLA_NOPE, gkn, 0.0)
    gknr = jnp.where(src >= MLA_NOPE, gkn, 0.0)

    q, k, vt, qr, kr, vr, gr, gates = _proj_call(
        x2, pos2, row(g_mix), _layout_w_in_mla(w_in, src), _w_in_rest_call(w_in),
        row(g_q_a), _layout_w_q_b(w_q_b, src),
        row(g_kv_a), _layout_w_kv_b(w_kv_b, src), gqn, gknn, gknr, inv)

    ot, (wa, wb, wo, wgu, wd) = _attn_call(
        q, k, vt, (w_mla_out, w_ret_out, w_out, w_gate_up, w_down), batch, seq)
    dec = jnp.stack([ret_decay_fwd, ret_decay_bwd]).astype(F32)
    ret = _ret_call(dec, qr, kr, vr, batch, seq)

    y = _out_call(x2, ot, ret, gr, gates, wa, wb, wo, row(g_ffn), wgu, wd)
    return y.reshape(batch, seq, d)
```

```python
import functools
import math

import numpy as np
import jax
import jax.numpy as jnp
from jax import lax
from jax.experimental import pallas as pl
from jax.experimental.pallas import tpu as pltpu

F32 = jnp.float32
BF16 = jnp.bfloat16

LANES = 128
VMEM_LIMIT_BYTES = 56 * 1024 * 1024

D_MODEL = 1024
MLA_HEADS = 8
MLA_Q_RANK = 256
MLA_KV_RANK = 128
MLA_NOPE = 64
MLA_ROPE = 32
MLA_V = 64
MLA_QK = MLA_NOPE + MLA_ROPE
RET_HEADS = 8
RET_QK = 64
RET_V = 128
FFN_HIDDEN = 2816
ROPE_THETA = 10000.0
EPS = 1e-6

W_IN_MLA_COLS = MLA_Q_RANK + MLA_KV_RANK + MLA_ROPE
OFF_QK_R = 0
OFF_V_R = OFF_QK_R + 2 * RET_HEADS * RET_QK
OFF_G_R = OFF_V_R + RET_HEADS * RET_V
OFF_GATE = OFF_G_R + RET_HEADS * RET_V
OFF_END = OFF_GATE + 2 * D_MODEL

TM_PROJ = 512
TM_OUT = 512
TQ = 512
TK = 256
S_AHEAD = 2
S_SLOTS = 4
Q_PER_BODY = 4
BF16_SUBLANES = 16
VT_ROWS = MLA_V + BF16_SUBLANES
RET_C = 256
RET_AHEAD = 2
RET_SLOTS = 4
FFN_CHUNK = 256


def _dot(a, b):
    return jnp.dot(a, b, preferred_element_type=F32)


def _rms_scale(v, n):
    return lax.rsqrt(jnp.sum(v * v, axis=-1, keepdims=True) * (1.0 / n) + EPS)


def _const_spec(shape):
    zeros = (0,) * len(shape)
    return pl.BlockSpec(shape, lambda *_: zeros, pipeline_mode=pl.Buffered(1))


def _proj_kernel(x_ref, pos_ref, gmix_ref, wsm_ref, win_ref, gqa_ref, wqb_ref, gkva_ref, wkvb_ref,
                 gqn_ref, gknn_ref, gknr_ref, inv_ref,
                 q_ref, k_ref, vt_ref, qr_ref, kr_ref, vr_ref, gr_ref, gate_ref, wal_ref):
    @pl.when(pl.program_id(0) == 0)
    def _():
        for j in range(OFF_END // LANES):
            c0 = W_IN_MLA_COLS + j * LANES
            wal_ref[:, j * LANES:(j + 1) * LANES] = win_ref[:, c0:c0 + LANES]

    x = x_ref[...]
    h = (x * _rms_scale(x, D_MODEL) * gmix_ref[...]).astype(BF16)

    pos = pos_ref[...].astype(F32)
    lane = lax.broadcasted_iota(jnp.int32, (1, LANES), 1)
    half = RET_QK // 2
    ang = pos * inv_ref[...]
    cs_r = jnp.cos(ang)
    sn = jnp.sin(ang)
    first_half = (lane % RET_QK) < half
    lo_r = jnp.where(first_half, -sn, 0.0)
    hi_r = jnp.where(first_half, 0.0, sn)
    mla_rope = ((lane % 2) == 0) & (lane < RET_QK)
    cs_m = jnp.where(mla_rope, cs_r, 1.0)
    lo_m = jnp.where(mla_rope, lo_r, 0.0)
    hi_m = jnp.where(mla_rope, hi_r, 0.0)

    def rope(v, cs, lo, hi):
        return (v * cs + pltpu.roll(v, LANES - half, 1) * lo + pltpu.roll(v, half, 1) * hi)

    rope_m = functools.partial(rope, cs=cs_m, lo=lo_m, hi=hi_m)
    rope_r = functools.partial(rope, cs=cs_r, lo=lo_r, hi=hi_r)

    small = _dot(h, wsm_ref[...])
    cq = small[:, :MLA_Q_RANK]
    ckv = small[:, MLA_Q_RANK:MLA_Q_RANK + MLA_KV_RANK]
    krb = small[:, MLA_Q_RANK + MLA_KV_RANK:]

    cqn = (cq * _rms_scale(cq, MLA_Q_RANK) * gqa_ref[...]).astype(BF16)
    qf = _dot(cqn, wqb_ref[...])
    gqn = gqn_ref[...]
    gate_ref[:, :D_MODEL] = _dot(h, wal_ref[:, OFF_GATE:OFF_GATE + D_MODEL]).astype(BF16)
    for hd in range(MLA_HEADS):
        blk = qf[:, hd * LANES:(hd + 1) * LANES]
        qn = blk * _rms_scale(blk, MLA_QK) * gqn
        q_ref[:, hd * LANES:(hd + 1) * LANES] = rope_m(qn).astype(BF16)

    ckvn = (ckv * _rms_scale(ckv, MLA_KV_RANK) * gkva_ref[...]).astype(BF16)
    kvf = _dot(ckvn, wkvb_ref[...])
    vr_ref[...] = _dot(h, wal_ref[:, OFF_V_R:OFF_G_R]).astype(BF16)
    kr_roped = rope_m(krb * gknr_ref[...])
    ss_rope = jnp.sum(krb * krb, axis=-1, keepdims=True)
    gknn = gknn_ref[...]
    for hd in range(MLA_HEADS):
        blk = kvf[:, hd * LANES:(hd + 1) * LANES]
        ss = jnp.sum(blk * blk, axis=-1, keepdims=True) + ss_rope
        r = lax.rsqrt(ss * (1.0 / MLA_QK) + EPS)
        k_ref[:, hd * LANES:(hd + 1) * LANES] = ((blk * gknn + kr_roped) * r).astype(BF16)
    gr_ref[...] = _dot(h, wal_ref[:, OFF_G_R:OFF_GATE]).astype(BF16)
    vt = kvf[:, MLA_HEADS * LANES:].T.astype(BF16)
    ones_rows = jnp.where(lax.broadcasted_iota(jnp.int32, (BF16_SUBLANES, vt.shape[1]), 0) == 0,
                          1.0, 0.0).astype(BF16)
    for hd in range(MLA_HEADS):
        vt_ref[hd * VT_ROWS:hd * VT_ROWS + MLA_V, :] = vt[hd * MLA_V:(hd + 1) * MLA_V]
        vt_ref[hd * VT_ROWS + MLA_V:(hd + 1) * VT_ROWS, :] = ones_rows

    qk_r = _dot(h, wal_ref[:, OFF_QK_R:OFF_V_R])
    gate_ref[:, D_MODEL:] = _dot(h, wal_ref[:, OFF_GATE + D_MODEL:OFF_END]).astype(BF16)
    k_off = RET_HEADS * RET_QK
    for j in range(k_off // LANES):
        qr_ref[:, j * LANES:(j + 1) * LANES] = rope_r(qk_r[:, j * LANES:(j + 1) * LANES]).astype(BF16)
        kb = qk_r[:, k_off + j * LANES:k_off + (j + 1) * LANES]
        kr_ref[:, j * LANES:(j + 1) * LANES] = (rope_r(kb) * (RET_QK ** -0.5)).astype(BF16)


def _proj_call(x2, pos2, gmix, wsm, win, gqa, wqb, gkva, wkvb, gqn, gknn, gknr, inv):
    t = x2.shape[0]
    tm = TM_PROJ
    row = lambda w: pl.BlockSpec((tm, w), lambda i: (i, 0))
    out_shape = (
        jax.ShapeDtypeStruct((t, MLA_HEADS * LANES), BF16),
        jax.ShapeDtypeStruct((t, MLA_HEADS * LANES), BF16),
        jax.ShapeDtypeStruct((MLA_HEADS * VT_ROWS, t), BF16),
        jax.ShapeDtypeStruct((t, RET_HEADS * RET_QK), BF16),
        jax.ShapeDtypeStruct((t, RET_HEADS * RET_QK), BF16),
        jax.ShapeDtypeStruct((t, RET_HEADS * RET_V), BF16),
        jax.ShapeDtypeStruct((t, RET_HEADS * RET_V), BF16),
        jax.ShapeDtypeStruct((t, 2 * D_MODEL), BF16),
    )
    return pl.pallas_call(
        _proj_kernel,
        grid=(t // tm,),
        in_specs=[
            row(D_MODEL), row(1),
            _const_spec(gmix.shape), _const_spec(wsm.shape), _const_spec(win.shape),
            _const_spec(gqa.shape),
            _const_spec(wqb.shape), _const_spec(gkva.shape), _const_spec(wkvb.shape),
            _const_spec(gqn.shape), _const_spec(gknn.shape), _const_spec(gknr.shape),
            _const_spec(inv.shape),
        ],
        out_specs=(
            row(MLA_HEADS * LANES), row(MLA_HEADS * LANES),
            pl.BlockSpec((MLA_HEADS * VT_ROWS, tm), lambda i: (0, i)),
            row(RET_HEADS * RET_QK), row(RET_HEADS * RET_QK),
            row(RET_HEADS * RET_V), row(RET_HEADS * RET_V), row(2 * D_MODEL),
        ),
        out_shape=out_shape,
        scratch_shapes=[pltpu.VMEM((D_MODEL, OFF_END), BF16)],
        compiler_params=pltpu.CompilerParams(
            dimension_semantics=("arbitrary",), vmem_limit_bytes=VMEM_LIMIT_BYTES),
        name="proj_in",
    )(x2, pos2, gmix, wsm, win, gqa, wqb, gkva, wkvb, gqn, gknn, gknr, inv)


def _attn_kernel(q_ref, k_ref, vt_ref, *refs, seq, n_w):
    o_ref, s_ref = refs[n_w], refs[-1]
    for w_ref, w_bf_ref in zip(refs[:n_w], refs[n_w + 1:-1]):
        w_bf_ref[...] = w_ref[...].astype(BF16)

    n_q = seq // TQ
    n_k = seq // TK
    assert n_k % S_SLOTS == 0 and n_k > S_AHEAD

    def q_transposed(qi):
        q0 = pl.multiple_of(qi * TQ, TQ)
        return q_ref[pl.ds(q0, TQ), :].astype(F32).T.astype(BF16)

    def scores(qt, ki):
        s = _dot(k_ref[ki * TK:(ki + 1) * TK, :], qt)
        s_ref[ki % S_SLOTS, :, :TQ] = s
        return jnp.max(s, axis=0, keepdims=True)

    def q_tile(qi, carry):
        qt, cms = carry
        qt_next = q_transposed(jnp.minimum(qi + 1, n_q - 1))
        m, cms = cms[0], cms[1:]
        acc = jnp.zeros((VT_ROWS, TQ), F32)
        for ki in range(n_k):
            ka = ki + S_AHEAD
            cms = cms + (scores(qt, ka) if ka < n_k else scores(qt_next, ka - n_k),)
            p = jnp.exp2(s_ref[ki % S_SLOTS, :, :TQ] - m).astype(BF16)
            acc = acc + _dot(vt_ref[:, ki * TK:(ki + 1) * TK], p)
            if ki + 1 < n_k:
                m_new = jnp.maximum(m, cms[0])
                acc = acc * jnp.exp2(m - m_new)
                m, cms = m_new, cms[1:]
        q0 = pl.multiple_of(qi * TQ, TQ)
        o_ref[:, pl.ds(q0, TQ)] = (acc[:MLA_V] / acc[MLA_V:MLA_V + 1]).astype(BF16)
        return qt_next, cms

    def q_tiles(j, carry):
        for u in range(Q_PER_BODY):
            carry = q_tile(j * Q_PER_BODY + u, carry)
        return carry

    assert n_q % Q_PER_BODY == 0
    qt0 = q_transposed(0)
    lax.fori_loop(0, n_q // Q_PER_BODY, q_tiles,
                  (qt0, tuple(scores(qt0, i) for i in range(S_AHEAD))))


def _attn_call(q, k, vt, weights, batch, seq):
    t = batch * seq
    steps = batch * MLA_HEADS
    slabs = []
    for w in weights:
        rows = w.shape[0] // steps
        assert rows * steps == w.shape[0] and rows % BF16_SUBLANES == 0
        slabs.append(pl.BlockSpec((rows, w.shape[1]), lambda b, h: (b * MLA_HEADS + h, 0)))
    out = pl.pallas_call(
        functools.partial(_attn_kernel, seq=seq, n_w=len(weights)),
        grid=(batch, MLA_HEADS),
        in_specs=[
            pl.BlockSpec((seq, LANES), lambda b, h: (b, h)),
            pl.BlockSpec((seq, LANES), lambda b, h: (b, h)),
            pl.BlockSpec((VT_ROWS, seq), lambda b, h: (h, b)),
            *slabs,
        ],
        out_specs=(pl.BlockSpec((MLA_V, seq), lambda b, h: (h, b)), *slabs),
        out_shape=(jax.ShapeDtypeStruct((MLA_HEADS * MLA_V, t), BF16),
                   *(jax.ShapeDtypeStruct(w.shape, BF16) for w in weights)),
        scratch_shapes=[pltpu.VMEM((S_SLOTS, TK, TQ + LANES), F32)],
        compiler_params=pltpu.CompilerParams(
            dimension_semantics=("arbitrary", "arbitrary"), vmem_limit_bytes=VMEM_LIMIT_BYTES),
        name="mla_attention",
    )(q, k, vt, *weights)
    return out[0], out[1:]


def _ret_kernel(dec_ref, q_ref, k_ref, v_ref, o_ref, kv_ref, st_ref, a_ref, *, seq):
    c = RET_C
    n = seq // c
    hd = pl.program_id(1)
    lo = (hd % 2) * RET_QK
    lane = lax.broadcasted_iota(jnp.int32, (1, LANES), 1)
    own = (lane >= lo) & (lane < lo + RET_QK)
    row = lax.broadcasted_iota(jnp.int32, (LANES, 1), 0)
    own_row = (row >= lo) & (row < lo + RET_QK)
    own_bf = jnp.where(own, 1.0, 0.0).astype(BF16)

    lg_f = -jnp.exp(jnp.full((1, 1), dec_ref[0, hd], F32))
    lg_b = -jnp.exp(jnp.full((1, 1), dec_ref[1, hd], F32))
    a = lax.broadcasted_iota(jnp.int32, (c, 1), 0).astype(F32)
    b = lax.broadcasted_iota(jnp.int32, (1, c), 1).astype(F32)
    q_decay = jnp.where(own, jnp.exp(lg_f * (a + 1.0)), jnp.exp(lg_b * (c - a)))
    k_decay = jnp.where(own, jnp.exp(lg_f * (c - 1.0 - a)), jnp.exp(lg_b * a))
    cd_f = jnp.exp(lg_f * float(c))
    cd_b = jnp.exp(lg_b * float(c))
    diff = a - b
    decay = jnp.where(diff >= 0, jnp.exp(lg_f * jnp.maximum(diff, 0.0)),
                      jnp.exp(lg_b * jnp.maximum(-diff, 0.0)))

    def both_halves(v):
        return jnp.where(own, v, pltpu.roll(v, RET_QK, 1))

    def chunk_kv(i, carry):
        r0 = pl.multiple_of(i * c, c)
        k = k_ref[pl.ds(r0, c), :].astype(F32)
        kv_ref[i] = _dot((both_halves(k) * k_decay).T.astype(BF16), v_ref[pl.ds(r0, c), :])
        return carry

    lax.fori_loop(0, n, chunk_kv, 0, unroll=True)

    def fwd_scan(i, f):
        st_ref[i] = f.astype(BF16)
        return f * cd_f + kv_ref[i]

    lax.fori_loop(0, n, fwd_scan, jnp.zeros((LANES, RET_V), F32))

    def bwd_scan(j, s):
        i = n - 1 - j
        st_ref[i] = jnp.where(own_row, st_ref[i].astype(F32), s).astype(BF16)
        return s * cd_b + kv_ref[i]

    lax.fori_loop(0, n, bwd_scan, jnp.zeros((LANES, RET_V), F32))

    def scores(i):
        rows = slice(i * c, (i + 1) * c)
        a_ref[i % RET_SLOTS] = lax.dot_general(
            q_ref[rows, :] * own_bf, k_ref[rows, :], (((1,), (1,)), ((), ())),
            preferred_element_type=F32)

    for i in range(min(RET_AHEAD, n)):
        scores(i)
    for i in range(n):
        if i + RET_AHEAD < n:
            scores(i + RET_AHEAD)
        rows = slice(i * c, (i + 1) * c)
        p = (a_ref[i % RET_SLOTS] * decay).astype(BF16)
        qd = (both_halves(q_ref[rows, :].astype(F32)) * q_decay).astype(BF16)
        o = _dot(jnp.concatenate([p, qd], axis=1),
                 jnp.concatenate([v_ref[rows, :], st_ref[i]], axis=0))
        o_ref[rows, :] = (o * _rms_scale(o, RET_V)).astype(BF16)


def _ret_call(dec, qr, kr, vr, batch, seq):
    t = batch * seq
    pair = pl.BlockSpec((seq, LANES), lambda b, h: (b, h // 2))
    head = pl.BlockSpec((seq, LANES), lambda b, h: (b, h))
    return pl.pallas_call(
        functools.partial(_ret_kernel, seq=seq),
        grid=(batch, RET_HEADS),
        in_specs=[pl.BlockSpec(memory_space=pltpu.SMEM), pair, pair, head],
        out_specs=head,
        out_shape=jax.ShapeDtypeStruct((t, RET_HEADS * RET_V), BF16),
        scratch_shapes=[pltpu.VMEM((seq // RET_C, LANES, RET_V), F32),
                        pltpu.VMEM((seq // RET_C, LANES, RET_V), BF16),
                        pltpu.VMEM((RET_SLOTS, RET_C, RET_C), F32)],
        compiler_params=pltpu.CompilerParams(
            dimension_semantics=("arbitrary", "arbitrary"), vmem_limit_bytes=VMEM_LIMIT_BYTES),
        name="retention",
    )(dec, qr, kr, vr)


def _out_kernel(x_ref, ot_ref, ret_ref, gr_ref, gate_ref, wa_ref, wb_ref, wo_ref, gffn_ref,
                wgu_ref, wd_ref, y_ref, act_ref):
    o_a = ot_ref[...].astype(F32).T.astype(BF16)
    y_a = _dot(o_a, wa_ref[...])
    g = gr_ref[...].astype(F32)
    o_b = (g * jax.nn.sigmoid(g) * ret_ref[...].astype(F32)).astype(BF16)
    y_b = _dot(o_b, wb_ref[...])
    merged = (jax.nn.sigmoid(gate_ref[:, :D_MODEL].astype(F32)) * y_a
              + jax.nn.sigmoid(gate_ref[:, D_MODEL:].astype(F32)) * y_b).astype(BF16)
    x1 = x_ref[...] + _dot(merged, wo_ref[...])
    h2 = (x1 * _rms_scale(x1, D_MODEL) * gffn_ref[...]).astype(BF16)
    for cidx in range(FFN_HIDDEN // FFN_CHUNK):
        c0 = cidx * FFN_CHUNK
        gate = _dot(h2, wgu_ref[:, c0:c0 + FFN_CHUNK])
        up = _dot(h2, wgu_ref[:, FFN_HIDDEN + c0:FFN_HIDDEN + c0 + FFN_CHUNK])
        act_ref[:, c0:c0 + FFN_CHUNK] = (gate * jax.nn.sigmoid(gate) * up).astype(BF16)
    y_ref[...] = x1 + _dot(act_ref[...], wd_ref[...])


def _out_call(x2, ot, ret, gr, gates, wa, wb, wo, gffn, wgu, wd):
    t = x2.shape[0]
    tm = TM_OUT
    row = lambda w: pl.BlockSpec((tm, w), lambda i: (i, 0))
    return pl.pallas_call(
        _out_kernel,
        grid=(t // tm,),
        in_specs=[
            row(D_MODEL),
            pl.BlockSpec((MLA_HEADS * MLA_V, tm), lambda i: (0, i)),
            row(RET_HEADS * RET_V), row(RET_HEADS * RET_V), row(2 * D_MODEL),
            _const_spec(wa.shape), _const_spec(wb.shape), _const_spec(wo.shape),
            _const_spec(gffn.shape), _const_spec(wgu.shape), _const_spec(wd.shape),
        ],
        out_specs=row(D_MODEL),
        out_shape=jax.ShapeDtypeStruct((t, D_MODEL), F32),
        scratch_shapes=[pltpu.VMEM((tm, FFN_HIDDEN), BF16)],
        compiler_params=pltpu.CompilerParams(
            dimension_semantics=("arbitrary",), vmem_limit_bytes=VMEM_LIMIT_BYTES),
        name="merge_ffn",
    )(x2, ot, ret, gr, gates, wa, wb, wo, gffn, wgu, wd)


def _mla_head_lane_map():
    half = MLA_ROPE // 2
    src = np.full(LANES, MLA_QK, np.int32)
    for i in range(half):
        src[2 * i] = MLA_NOPE + i
        src[RET_QK // 2 + 2 * i] = MLA_NOPE + half + i
    for n in range(MLA_NOPE // 2):
        src[2 * n + 1] = n
        src[MLA_NOPE + n] = MLA_NOPE // 2 + n
    return src


def _take_lanes(w, src, n_valid):
    pad = [(0, 0)] * (w.ndim - 1) + [(0, 1)]
    return jnp.take(jnp.pad(w, pad), np.minimum(src, n_valid), axis=-1)


def _layout_w_in(w_in, src):
    latent = MLA_Q_RANK + MLA_KV_RANK
    rope_src = np.where(src >= MLA_NOPE, src - MLA_NOPE, MLA_ROPE)
    k_rope_blk = _take_lanes(w_in[:, latent:latent + MLA_ROPE], rope_src, MLA_ROPE)
    small = jnp.concatenate([w_in[:, :latent], k_rope_blk], axis=1).astype(BF16)
    return small, w_in.astype(BF16)


def _layout_w_q_b(w_q_b, src):
    w = w_q_b.reshape(MLA_Q_RANK, MLA_HEADS, MLA_QK)
    return _take_lanes(w, src, MLA_QK).reshape(MLA_Q_RANK, MLA_HEADS * LANES).astype(BF16)


def _layout_w_kv_b(w_kv_b, src):
    w = w_kv_b.reshape(MLA_KV_RANK, MLA_HEADS, MLA_NOPE + MLA_V)
    k_nope = _take_lanes(w[:, :, :MLA_NOPE], src, MLA_NOPE)
    v = w[:, :, MLA_NOPE:]
    return jnp.concatenate([k_nope.reshape(MLA_KV_RANK, MLA_HEADS * LANES),
                            v.reshape(MLA_KV_RANK, MLA_HEADS * MLA_V)], axis=1).astype(BF16)


def _rope_freqs():
    half = RET_QK // 2
    inv = ROPE_THETA ** (-(np.arange(LANES) % half) / half)
    return jnp.asarray(inv.reshape(1, LANES), F32)


def kernel(x, positions, g_mix, w_in, g_q_a, w_q_b, g_kv_a, w_kv_b, g_qn, g_kn, w_mla_out,
           ret_decay_fwd, ret_decay_bwd, w_ret_out, w_out, g_ffn, w_gate_up, w_down):
    batch, seq, d = x.shape
    t = batch * seq
    x2 = x.reshape(t, d)
    pos2 = positions.reshape(t, 1).astype(jnp.int32)
    inv = _rope_freqs()

    row = lambda v: v.astype(F32).reshape(1, -1)
    src = _mla_head_lane_map()
    gqn = _take_lanes(row(g_qn), src, MLA_QK) * (MLA_QK ** -0.5 * math.log2(math.e))
    gkn = _take_lanes(row(g_kn), src, MLA_QK)
    gknn = jnp.where(src < MLA_NOPE, gkn, 0.0)
    gknr = jnp.where(src >= MLA_NOPE, gkn, 0.0)

    q, k, vt, qr, kr, vr, gr, gates = _proj_call(
        x2, pos2, row(g_mix), *_layout_w_in(w_in, src), row(g_q_a), _layout_w_q_b(w_q_b, src),
        row(g_kv_a), _layout_w_kv_b(w_kv_b, src), gqn, gknn, gknr, inv)

    ot, (wa, wb, wo, wgu, wd) = _attn_call(
        q, k, vt, (w_mla_out, w_ret_out, w_out, w_gate_up, w_down), batch, seq)
    dec = jnp.stack([ret_decay_fwd, ret_decay_bwd]).astype(F32)
    ret = _ret_call(dec, qr, kr, vr, batch, seq)

    y = _out_call(x2, ot, ret, gr, gates, wa, wb, wo, row(g_ffn), wgu, wd)
    return y.reshape(batch, seq, d)
```

```python
import functools
import math

import numpy as np
import jax
import jax.numpy as jnp
from jax import lax
from jax.experimental import pallas as pl
from jax.experimental.pallas import tpu as pltpu

F32 = jnp.float32
BF16 = jnp.bfloat16

LANES = 128
VMEM_LIMIT_BYTES = 56 * 1024 * 1024

D_MODEL = 1024
MLA_HEADS = 8
MLA_Q_RANK = 256
MLA_KV_RANK = 128
MLA_NOPE = 64
MLA_ROPE = 32
MLA_V = 64
MLA_QK = MLA_NOPE + MLA_ROPE
RET_HEADS = 8
RET_QK = 64
RET_V = 128
FFN_HIDDEN = 2816
ROPE_THETA = 10000.0
EPS = 1e-6

W_IN_MLA_COLS = MLA_Q_RANK + MLA_KV_RANK + MLA_ROPE
OFF_QK_R = 0
OFF_V_R = OFF_QK_R + 2 * RET_HEADS * RET_QK
OFF_G_R = OFF_V_R + RET_HEADS * RET_V
OFF_GATE = OFF_G_R + RET_HEADS * RET_V
OFF_END = OFF_GATE + 2 * D_MODEL

TM_PROJ = 512
TM_OUT = 512
TQ = 512
TK = 256
S_AHEAD = 2
S_SLOTS = 4
Q_PER_BODY = 4
BF16_SUBLANES = 16
VT_ROWS = MLA_V + BF16_SUBLANES
RET_C = 256
RET_AHEAD = 2
RET_SLOTS = 4
FFN_CHUNK = 256


def _dot(a, b):
    return jnp.dot(a, b, preferred_element_type=F32)


def _rms_scale(v, n):
    return lax.rsqrt(jnp.sum(v * v, axis=-1, keepdims=True) * (1.0 / n) + EPS)


def _const_spec(shape):
    zeros = (0,) * len(shape)
    return pl.BlockSpec(shape, lambda *_: zeros, pipeline_mode=pl.Buffered(1))


def _proj_kernel(x_ref, pos_ref, gmix_ref, wsm_ref, win_ref, gqa_ref, wqb_ref, gkva_ref, wkvb_ref,
                 gqn_ref, gknn_ref, gknr_ref, inv_ref,
                 q_ref, k_ref, vt_ref, qr_ref, kr_ref, vr_ref, gr_ref, gate_ref, wal_ref):
    @pl.when(pl.program_id(0) == 0)
    def _():
        for j in range(OFF_END // LANES):
            c0 = W_IN_MLA_COLS + j * LANES
            wal_ref[:, j * LANES:(j + 1) * LANES] = win_ref[:, c0:c0 + LANES]

    x = x_ref[...]
    h = (x * _rms_scale(x, D_MODEL) * gmix_ref[...]).astype(BF16)

    pos = pos_ref[...].astype(F32)
    lane = lax.broadcasted_iota(jnp.int32, (1, LANES), 1)
    half = RET_QK // 2
    ang = pos * inv_ref[...]
    cs_r = jnp.cos(ang)
    sn = jnp.sin(ang)
    first_half = (lane % RET_QK) < half
    lo_r = jnp.where(first_half, -sn, 0.0)
    hi_r = jnp.where(first_half, 0.0, sn)
    mla_rope = ((lane % 2) == 0) & (lane < RET_QK)
    cs_m = jnp.where(mla_rope, cs_r, 1.0)
    lo_m = jnp.where(mla_rope, lo_r, 0.0)
    hi_m = jnp.where(mla_rope, hi_r, 0.0)

    def rope(v, cs, lo, hi):
        return (v * cs + pltpu.roll(v, LANES - half, 1) * lo + pltpu.roll(v, half, 1) * hi)

    rope_m = functools.partial(rope, cs=cs_m, lo=lo_m, hi=hi_m)
    rope_r = functools.partial(rope, cs=cs_r, lo=lo_r, hi=hi_r)

    small = _dot(h, wsm_ref[...])
    cq = small[:, :MLA_Q_RANK]
    ckv = small[:, MLA_Q_RANK:MLA_Q_RANK + MLA_KV_RANK]
    krb = small[:, MLA_Q_RANK + MLA_KV_RANK:]

    cqn = (cq * _rms_scale(cq, MLA_Q_RANK) * gqa_ref[...]).astype(BF16)
    qf = _dot(cqn, wqb_ref[...])
    gqn = gqn_ref[...]
    gate_ref[:, :D_MODEL] = _dot(h, wal_ref[:, OFF_GATE:OFF_GATE + D_MODEL]).astype(BF16)
    for hd in range(MLA_HEADS):
        blk = qf[:, hd * LANES:(hd + 1) * LANES]
        qn = blk * _rms_scale(blk, MLA_QK) * gqn
        q_ref[:, hd * LANES:(hd + 1) * LANES] = rope_m(qn).astype(BF16)

    ckvn = (ckv * _rms_scale(ckv, MLA_KV_RANK) * gkva_ref[...]).astype(BF16)
    kvf = _dot(ckvn, wkvb_ref[...])
    vr_ref[...] = _dot(h, wal_ref[:, OFF_V_R:OFF_G_R]).astype(BF16)
    kr_roped = rope_m(krb * gknr_ref[...])
    ss_rope = jnp.sum(krb * krb, axis=-1, keepdims=True)
    gknn = gknn_ref[...]
    for hd in range(MLA_HEADS):
        blk = kvf[:, hd * LANES:(hd + 1) * LANES]
        ss = jnp.sum(blk * blk, axis=-1, keepdims=True) + ss_rope
        r = lax.rsqrt(ss * (1.0 / MLA_QK) + EPS)
        k_ref[:, hd * LANES:(hd + 1) * LANES] = ((blk * gknn + kr_roped) * r).astype(BF16)
    gr_ref[...] = _dot(h, wal_ref[:, OFF_G_R:OFF_GATE]).astype(BF16)
    vt = kvf[:, MLA_HEADS * LANES:].T.astype(BF16)
    ones_rows = jnp.where(lax.broadcasted_iota(jnp.int32, (BF16_SUBLANES, vt.shape[1]), 0) == 0,
                          1.0, 0.0).astype(BF16)
    for hd in range(MLA_HEADS):
        vt_ref[hd * VT_ROWS:hd * VT_ROWS + MLA_V, :] = vt[hd * MLA_V:(hd + 1) * MLA_V]
        vt_ref[hd * VT_ROWS + MLA_V:(hd + 1) * VT_ROWS, :] = ones_rows

    qk_r = _dot(h, wal_ref[:, OFF_QK_R:OFF_V_R])
    gate_ref[:, D_MODEL:] = _dot(h, wal_ref[:, OFF_GATE + D_MODEL:OFF_END]).astype(BF16)
    k_off = RET_HEADS * RET_QK
    for j in range(k_off // LANES):
        qr_ref[:, j * LANES:(j + 1) * LANES] = rope_r(qk_r[:, j * LANES:(j + 1) * LANES]).astype(BF16)
        kb = qk_r[:, k_off + j * LANES:k_off + (j + 1) * LANES]
        kr_ref[:, j * LANES:(j + 1) * LANES] = (rope_r(kb) * (RET_QK ** -0.5)).astype(BF16)


def _proj_call(x2, pos2, gmix, wsm, win, gqa, wqb, gkva, wkvb, gqn, gknn, gknr, inv):
    t = x2.shape[0]
    tm = TM_PROJ
    row = lambda w: pl.BlockSpec((tm, w), lambda i: (i, 0))
    out_shape = (
        jax.ShapeDtypeStruct((t, MLA_HEADS * LANES), BF16),
        jax.ShapeDtypeStruct((t, MLA_HEADS * LANES), BF16),
        jax.ShapeDtypeStruct((MLA_HEADS * VT_ROWS, t), BF16),
        jax.ShapeDtypeStruct((t, RET_HEADS * RET_QK), BF16),
        jax.ShapeDtypeStruct((t, RET_HEADS * RET_QK), BF16),
        jax.ShapeDtypeStruct((t, RET_HEADS * RET_V), BF16),
        jax.ShapeDtypeStruct((t, RET_HEADS * RET_V), BF16),
        jax.ShapeDtypeStruct((t, 2 * D_MODEL), BF16),
    )
    return pl.pallas_call(
        _proj_kernel,
        grid=(t // tm,),
        in_specs=[
            row(D_MODEL), row(1),
            _const_spec(gmix.shape), _const_spec(wsm.shape), _const_spec(win.shape),
            _const_spec(gqa.shape),
            _const_spec(wqb.shape), _const_spec(gkva.shape), _const_spec(wkvb.shape),
            _const_spec(gqn.shape), _const_spec(gknn.shape), _const_spec(gknr.shape),
            _const_spec(inv.shape),
        ],
        out_specs=(
            row(MLA_HEADS * LANES), row(MLA_HEADS * LANES),
            pl.BlockSpec((MLA_HEADS * VT_ROWS, tm), lambda i: (0, i)),
            row(RET_HEADS * RET_QK), row(RET_HEADS * RET_QK),
            row(RET_HEADS * RET_V), row(RET_HEADS * RET_V), row(2 * D_MODEL),
        ),
        out_shape=out_shape,
        scratch_shapes=[pltpu.VMEM((D_MODEL, OFF_END), BF16)],
        compiler_params=pltpu.CompilerParams(
            dimension_semantics=("arbitrary",), vmem_limit_bytes=VMEM_LIMIT_BYTES),
        name="proj_in",
    )(x2, pos2, gmix, wsm, win, gqa, wqb, gkva, wkvb, gqn, gknn, gknr, inv)


def _attn_kernel(q_ref, k_ref, vt_ref, *refs, seq, n_w):
    o_ref, s_ref = refs[n_w], refs[-1]
    for w_ref, w_bf_ref in zip(refs[:n_w], refs[n_w + 1:-1]):
        w_bf_ref[...] = w_ref[...].astype(BF16)

    n_q = seq // TQ
    n_k = seq // TK
    assert n_k % S_SLOTS == 0 and n_k > S_AHEAD

    def q_transposed(qi):
        q0 = pl.multiple_of(qi * TQ, TQ)
        return q_ref[pl.ds(q0, TQ), :].astype(F32).T.astype(BF16)

    def scores(qt, ki):
        s = _dot(k_ref[ki * TK:(ki + 1) * TK, :], qt)
        s_ref[ki % S_SLOTS] = s
        return jnp.max(s, axis=0, keepdims=True)

    def q_tile(qi, carry):
        qt, cms = carry
        qt_next = q_transposed(jnp.minimum(qi + 1, n_q - 1))
        m, cms = cms[0], cms[1:]
        acc = jnp.zeros((VT_ROWS, TQ), F32)
        for ki in range(n_k):
            ka = ki + S_AHEAD
            cms = cms + (scores(qt, ka) if ka < n_k else scores(qt_next, ka - n_k),)
            p = jnp.exp2(s_ref[ki % S_SLOTS] - m).astype(BF16)
            acc = acc + _dot(vt_ref[:, ki * TK:(ki + 1) * TK], p)
            if ki + 1 < n_k:
                m_new = jnp.maximum(m, cms[0])
                acc = acc * jnp.exp2(m - m_new)
                m, cms = m_new, cms[1:]
        q0 = pl.multiple_of(qi * TQ, TQ)
        o_ref[:, pl.ds(q0, TQ)] = (acc[:MLA_V] / acc[MLA_V:MLA_V + 1]).astype(BF16)
        return qt_next, cms

    def q_tiles(j, carry):
        for u in range(Q_PER_BODY):
            carry = q_tile(j * Q_PER_BODY + u, carry)
        return carry

    assert n_q % Q_PER_BODY == 0
    qt0 = q_transposed(0)
    lax.fori_loop(0, n_q // Q_PER_BODY, q_tiles,
                  (qt0, tuple(scores(qt0, i) for i in range(S_AHEAD))))


def _attn_call(q, k, vt, weights, batch, seq):
    t = batch * seq
    steps = batch * MLA_HEADS
    slabs = []
    for w in weights:
        rows = w.shape[0] // steps
        assert rows * steps == w.shape[0] and rows % BF16_SUBLANES == 0
        slabs.append(pl.BlockSpec((rows, w.shape[1]), lambda b, h: (b * MLA_HEADS + h, 0)))
    out = pl.pallas_call(
        functools.partial(_attn_kernel, seq=seq, n_w=len(weights)),
        grid=(batch, MLA_HEADS),
        in_specs=[
            pl.BlockSpec((seq, LANES), lambda b, h: (b, h)),
            pl.BlockSpec((seq, LANES), lambda b, h: (b, h)),
            pl.BlockSpec((VT_ROWS, seq), lambda b, h: (h, b)),
            *slabs,
        ],
        out_specs=(pl.BlockSpec((MLA_V, seq), lambda b, h: (h, b)), *slabs),
        out_shape=(jax.ShapeDtypeStruct((MLA_HEADS * MLA_V, t), BF16),
                   *(jax.ShapeDtypeStruct(w.shape, BF16) for w in weights)),
        scratch_shapes=[pltpu.VMEM((S_SLOTS, TK, TQ), F32)],
        compiler_params=pltpu.CompilerParams(
            dimension_semantics=("arbitrary", "arbitrary"), vmem_limit_bytes=VMEM_LIMIT_BYTES),
        name="mla_attention",
    )(q, k, vt, *weights)
    return out[0], out[1:]


def _ret_kernel(dec_ref, q_ref, k_ref, v_ref, o_ref, kv_ref, st_ref, a_ref, *, seq):
    c = RET_C
    n = seq // c
    hd = pl.program_id(1)
    lo = (hd % 2) * RET_QK
    lane = lax.broadcasted_iota(jnp.int32, (1, LANES), 1)
    own = (lane >= lo) & (lane < lo + RET_QK)
    row = lax.broadcasted_iota(jnp.int32, (LANES, 1), 0)
    own_row = (row >= lo) & (row < lo + RET_QK)
    own_bf = jnp.where(own, 1.0, 0.0).astype(BF16)

    lg_f = -jnp.exp(jnp.full((1, 1), dec_ref[0, hd], F32))
    lg_b = -jnp.exp(jnp.full((1, 1), dec_ref[1, hd], F32))
    a = lax.broadcasted_iota(jnp.int32, (c, 1), 0).astype(F32)
    b = lax.broadcasted_iota(jnp.int32, (1, c), 1).astype(F32)
    q_decay = jnp.where(own, jnp.exp(lg_f * (a + 1.0)), jnp.exp(lg_b * (c - a)))
    k_decay = jnp.where(own, jnp.exp(lg_f * (c - 1.0 - a)), jnp.exp(lg_b * a))
    cd_f = jnp.exp(lg_f * float(c))
    cd_b = jnp.exp(lg_b * float(c))
    diff = a - b
    decay = jnp.where(diff >= 0, jnp.exp(lg_f * jnp.maximum(diff, 0.0)),
                      jnp.exp(lg_b * jnp.maximum(-diff, 0.0)))

    def both_halves(v):
        return jnp.where(own, v, pltpu.roll(v, RET_QK, 1))

    def chunk_kv(i, carry):
        r0 = pl.multiple_of(i * c, c)
        k = k_ref[pl.ds(r0, c), :].astype(F32)
        kv_ref[i] = _dot((both_halves(k) * k_decay).T.astype(BF16), v_ref[pl.ds(r0, c), :])
        return carry

    lax.fori_loop(0, n, chunk_kv, 0, unroll=True)

    def fwd_scan(i, f):
        st_ref[i] = f.astype(BF16)
        return f * cd_f + kv_ref[i]

    lax.fori_loop(0, n, fwd_scan, jnp.zeros((LANES, RET_V), F32))

    def bwd_scan(j, s):
        i = n - 1 - j
        st_ref[i] = jnp.where(own_row, st_ref[i].astype(F32), s).astype(BF16)
        return s * cd_b + kv_ref[i]

    lax.fori_loop(0, n, bwd_scan, jnp.zeros((LANES, RET_V), F32))

    def scores(i):
        rows = slice(i * c, (i + 1) * c)
        a_ref[i % RET_SLOTS] = lax.dot_general(
            q_ref[rows, :] * own_bf, k_ref[rows, :], (((1,), (1,)), ((), ())),
            preferred_element_type=F32)

    for i in range(min(RET_AHEAD, n)):
        scores(i)
    for i in range(n):
        if i + RET_AHEAD < n:
            scores(i + RET_AHEAD)
        rows = slice(i * c, (i + 1) * c)
        p = (a_ref[i % RET_SLOTS] * decay).astype(BF16)
        qd = (both_halves(q_ref[rows, :].astype(F32)) * q_decay).astype(BF16)
        o = _dot(jnp.concatenate([p, qd], axis=1),
                 jnp.concatenate([v_ref[rows, :], st_ref[i]], axis=0))
        o_ref[rows, :] = (o * _rms_scale(o, RET_V)).astype(BF16)


def _ret_call(dec, qr, kr, vr, batch, seq):
    t = batch * seq
    pair = pl.BlockSpec((seq, LANES), lambda b, h: (b, h // 2))
    head = pl.BlockSpec((seq, LANES), lambda b, h: (b, h))
    return pl.pallas_call(
        functools.partial(_ret_kernel, seq=seq),
        grid=(batch, RET_HEADS),
        in_specs=[pl.BlockSpec(memory_space=pltpu.SMEM), pair, pair, head],
        out_specs=head,
        out_shape=jax.ShapeDtypeStruct((t, RET_HEADS * RET_V), BF16),
        scratch_shapes=[pltpu.VMEM((seq // RET_C, LANES, RET_V), F32),
                        pltpu.VMEM((seq // RET_C, LANES, RET_V), BF16),
                        pltpu.VMEM((RET_SLOTS, RET_C, RET_C), F32)],
        compiler_params=pltpu.CompilerParams(
            dimension_semantics=("arbitrary", "arbitrary"), vmem_limit_bytes=VMEM_LIMIT_BYTES),
        name="retention",
    )(dec, qr, kr, vr)


def _out_kernel(x_ref, ot_ref, ret_ref, gr_ref, gate_ref, wa_ref, wb_ref, wo_ref, gffn_ref,
                wgu_ref, wd_ref, y_ref, act_ref):
    o_a = ot_ref[...].astype(F32).T.astype(BF16)
    y_a = _dot(o_a, wa_ref[...])
    g = gr_ref[...].astype(F32)
    o_b = (g * jax.nn.sigmoid(g) * ret_ref[...].astype(F32)).astype(BF16)
    y_b = _dot(o_b, wb_ref[...])
    merged = (jax.nn.sigmoid(gate_ref[:, :D_MODEL].astype(F32)) * y_a
              + jax.nn.sigmoid(gate_ref[:, D_MODEL:].astype(F32)) * y_b).astype(BF16)
    x1 = x_ref[...] + _dot(merged, wo_ref[...])
    h2 = (x1 * _rms_scale(x1, D_MODEL) * gffn_ref[...]).astype(BF16)
    for cidx in range(FFN_HIDDEN // FFN_CHUNK):
        c0 = cidx * FFN_CHUNK
        gate = _dot(h2, wgu_ref[:, c0:c0 + FFN_CHUNK])
        up = _dot(h2, wgu_ref[:, FFN_HIDDEN + c0:FFN_HIDDEN + c0 + FFN_CHUNK])
        act_ref[:, c0:c0 + FFN_CHUNK] = (gate * jax.nn.sigmoid(gate) * up).astype(BF16)
    y_ref[...] = x1 + _dot(act_ref[...], wd_ref[...])


def _out_call(x2, ot, ret, gr, gates, wa, wb, wo, gffn, wgu, wd):
    t = x2.shape[0]
    tm = TM_OUT
    row = lambda w: pl.BlockSpec((tm, w), lambda i: (i, 0))
    return pl.pallas_call(
        _out_kernel,
        grid=(t // tm,),
        in_specs=[
            row(D_MODEL),
            pl.BlockSpec((MLA_HEADS * MLA_V, tm), lambda i: (0, i)),
            row(RET_HEADS * RET_V), row(RET_HEADS * RET_V), row(2 * D_MODEL),
            _const_spec(wa.shape), _const_spec(wb.shape), _const_spec(wo.shape),
            _const_spec(gffn.shape), _const_spec(wgu.shape), _const_spec(wd.shape),
        ],
        out_specs=row(D_MODEL),
        out_shape=jax.ShapeDtypeStruct((t, D_MODEL), F32),
        scratch_shapes=[pltpu.VMEM((tm, FFN_HIDDEN), BF16)],
        compiler_params=pltpu.CompilerParams(
            dimension_semantics=("arbitrary",), vmem_limit_bytes=VMEM_LIMIT_BYTES),
        name="merge_ffn",
    )(x2, ot, ret, gr, gates, wa, wb, wo, gffn, wgu, wd)


def _mla_head_lane_map():
    half = MLA_ROPE // 2
    src = np.full(LANES, MLA_QK, np.int32)
    for i in range(half):
        src[2 * i] = MLA_NOPE + i
        src[RET_QK // 2 + 2 * i] = MLA_NOPE + half + i
    for n in range(MLA_NOPE // 2):
        src[2 * n + 1] = n
        src[MLA_NOPE + n] = MLA_NOPE // 2 + n
    return src


def _take_lanes(w, src, n_valid):
    pad = [(0, 0)] * (w.ndim - 1) + [(0, 1)]
    return jnp.take(jnp.pad(w, pad), np.minimum(src, n_valid), axis=-1)


def _layout_w_in(w_in, src):
    latent = MLA_Q_RANK + MLA_KV_RANK
    rope_src = np.where(src >= MLA_NOPE, src - MLA_NOPE, MLA_ROPE)
    k_rope_blk = _take_lanes(w_in[:, latent:latent + MLA_ROPE], rope_src, MLA_ROPE)
    small = jnp.concatenate([w_in[:, :latent], k_rope_blk], axis=1).astype(BF16)
    return small, w_in.astype(BF16)


def _layout_w_q_b(w_q_b, src):
    w = w_q_b.reshape(MLA_Q_RANK, MLA_HEADS, MLA_QK)
    return _take_lanes(w, src, MLA_QK).reshape(MLA_Q_RANK, MLA_HEADS * LANES).astype(BF16)


def _layout_w_kv_b(w_kv_b, src):
    w = w_kv_b.reshape(MLA_KV_RANK, MLA_HEADS, MLA_NOPE + MLA_V)
    k_nope = _take_lanes(w[:, :, :MLA_NOPE], src, MLA_NOPE)
    v = w[:, :, MLA_NOPE:]
    return jnp.concatenate([k_nope.reshape(MLA_KV_RANK, MLA_HEADS * LANES),
                            v.reshape(MLA_KV_RANK, MLA_HEADS * MLA_V)], axis=1).astype(BF16)


def _rope_freqs():
    half = RET_QK // 2
    inv = ROPE_THETA ** (-(np.arange(LANES) % half) / half)
    return jnp.asarray(inv.reshape(1, LANES), F32)


def kernel(x, positions, g_mix, w_in, g_q_a, w_q_b, g_kv_a, w_kv_b, g_qn, g_kn, w_mla_out,
           ret_decay_fwd, ret_decay_bwd, w_ret_out, w_out, g_ffn, w_gate_up, w_down):
    batch, seq, d = x.shape
    t = batch * seq
    x2 = x.reshape(t, d)
    pos2 = positions.reshape(t, 1).astype(jnp.int32)
    inv = _rope_freqs()

    row = lambda v: v.astype(F32).reshape(1, -1)
    src = _mla_head_lane_map()
    gqn = _take_lanes(row(g_qn), src, MLA_QK) * (MLA_QK ** -0.5 * math.log2(math.e))
    gkn = _take_lanes(row(g_kn), src, MLA_QK)
    gknn = jnp.where(src < MLA_NOPE, gkn, 0.0)
    gknr = jnp.where(src >= MLA_NOPE, gkn, 0.0)

    q, k, vt, qr, kr, vr, gr, gates = _proj_call(
        x2, pos2, row(g_mix), *_layout_w_in(w_in, src), row(g_q_a), _layout_w_q_b(w_q_b, src),
        row(g_kv_a), _layout_w_kv_b(w_kv_b, src), gqn, gknn, gknr, inv)

    ot, (wa, wb, wo, wgu, wd) = _attn_call(
        q, k, vt, (w_mla_out, w_ret_out, w_out, w_gate_up, w_down), batch, seq)
    dec = jnp.stack([ret_decay_fwd, ret_decay_bwd]).astype(F32)
    ret = _ret_call(dec, qr, kr, vr, batch, seq)

    y = _out_call(x2, ot, ret, gr, gates, wa, wb, wo, row(g_ffn), wgu, wd)
    return y.reshape(batch, seq, d)
```

```python
import functools
import math

import numpy as np
import jax
import jax.numpy as jnp
from jax import lax
from jax.experimental import pallas as pl
from jax.experimental.pallas import tpu as pltpu

F32 = jnp.float32
BF16 = jnp.bfloat16

LANES = 128
VMEM_LIMIT_BYTES = 56 * 1024 * 1024

D_MODEL = 1024
MLA_HEADS = 8
MLA_Q_RANK = 256
MLA_KV_RANK = 128
MLA_NOPE = 64
MLA_ROPE = 32
MLA_V = 64
MLA_QK = MLA_NOPE + MLA_ROPE
RET_HEADS = 8
RET_QK = 64
RET_V = 128
FFN_HIDDEN = 2816
ROPE_THETA = 10000.0
EPS = 1e-6

W_IN_MLA_COLS = MLA_Q_RANK + MLA_KV_RANK + MLA_ROPE
OFF_QK_R = 0
OFF_V_R = OFF_QK_R + 2 * RET_HEADS * RET_QK
OFF_G_R = OFF_V_R + RET_HEADS * RET_V
OFF_GATE = OFF_G_R + RET_HEADS * RET_V
OFF_END = OFF_GATE + 2 * D_MODEL

TM_PROJ = 512
TM_OUT = 512
TQ = 512
TK = 256
S_AHEAD = 2
S_SLOTS = 4
Q_PER_BODY = 4
BF16_SUBLANES = 16
VT_ROWS = MLA_V + BF16_SUBLANES
RET_C = 256
RET_AHEAD = 2
RET_SLOTS = 4
FFN_CHUNK = 256


def _dot(a, b):
    return jnp.dot(a, b, preferred_element_type=F32)


def _rms_scale(v, n):
    return lax.rsqrt(jnp.sum(v * v, axis=-1, keepdims=True) * (1.0 / n) + EPS)


def _const_spec(shape):
    zeros = (0,) * len(shape)
    return pl.BlockSpec(shape, lambda *_: zeros, pipeline_mode=pl.Buffered(1))


def _proj_kernel(x_ref, pos_ref, gmix_ref, wsm_ref, win_ref, gqa_ref, wqb_ref, gkva_ref, wkvb_ref,
                 gqn_ref, gknn_ref, gknr_ref, inv_ref,
                 q_ref, k_ref, vt_ref, qr_ref, kr_ref, vr_ref, gr_ref, gate_ref, wal_ref):
    @pl.when(pl.program_id(0) == 0)
    def _():
        for j in range(OFF_END // LANES):
            c0 = W_IN_MLA_COLS + j * LANES
            wal_ref[:, j * LANES:(j + 1) * LANES] = win_ref[:, c0:c0 + LANES]

    x = x_ref[...]
    h = (x * _rms_scale(x, D_MODEL) * gmix_ref[...]).astype(BF16)

    pos = pos_ref[...].astype(F32)
    lane = lax.broadcasted_iota(jnp.int32, (1, LANES), 1)
    half = RET_QK // 2
    ang = pos * inv_ref[...]
    cs_r = jnp.cos(ang)
    sn = jnp.sin(ang)
    first_half = (lane % RET_QK) < half
    lo_r = jnp.where(first_half, -sn, 0.0)
    hi_r = jnp.where(first_half, 0.0, sn)
    mla_rope = ((lane % 2) == 0) & (lane < RET_QK)
    cs_m = jnp.where(mla_rope, cs_r, 1.0)
    lo_m = jnp.where(mla_rope, lo_r, 0.0)
    hi_m = jnp.where(mla_rope, hi_r, 0.0)

    def rope(v, cs, lo, hi):
        return (v * cs + pltpu.roll(v, LANES - half, 1) * lo + pltpu.roll(v, half, 1) * hi)

    rope_m = functools.partial(rope, cs=cs_m, lo=lo_m, hi=hi_m)
    rope_r = functools.partial(rope, cs=cs_r, lo=lo_r, hi=hi_r)

    small = _dot(h, wsm_ref[...])
    cq = small[:, :MLA_Q_RANK]
    ckv = small[:, MLA_Q_RANK:MLA_Q_RANK + MLA_KV_RANK]
    krb = small[:, MLA_Q_RANK + MLA_KV_RANK:]

    cqn = (cq * _rms_scale(cq, MLA_Q_RANK) * gqa_ref[...]).astype(BF16)
    qf = _dot(cqn, wqb_ref[...])
    gqn = gqn_ref[...]
    gate_ref[:, :D_MODEL] = _dot(h, wal_ref[:, OFF_GATE:OFF_GATE + D_MODEL]).astype(BF16)
    for hd in range(MLA_HEADS):
        blk = qf[:, hd * LANES:(hd + 1) * LANES]
        qn = blk * _rms_scale(blk, MLA_QK) * gqn
        q_ref[:, hd * LANES:(hd + 1) * LANES] = rope_m(qn).astype(BF16)

    ckvn = (ckv * _rms_scale(ckv, MLA_KV_RANK) * gkva_ref[...]).astype(BF16)
    kvf = _dot(ckvn, wkvb_ref[...])
    vr_ref[...] = _dot(h, wal_ref[:, OFF_V_R:OFF_G_R]).astype(BF16)
    kr_roped = rope_m(krb * gknr_ref[...])
    ss_rope = jnp.sum(krb * krb, axis=-1, keepdims=True)
    gknn = gknn_ref[...]
    for hd in range(MLA_HEADS):
        blk = kvf[:, hd * LANES:(hd + 1) * LANES]
        ss = jnp.sum(blk * blk, axis=-1, keepdims=True) + ss_rope
        r = lax.rsqrt(ss * (1.0 / MLA_QK) + EPS)
        k_ref[:, hd * LANES:(hd + 1) * LANES] = ((blk * gknn + kr_roped) * r).astype(BF16)
    gr_ref[...] = _dot(h, wal_ref[:, OFF_G_R:OFF_GATE]).astype(BF16)
    vt = kvf[:, MLA_HEADS * LANES:].T.astype(BF16)
    ones_rows = jnp.where(lax.broadcasted_iota(jnp.int32, (BF16_SUBLANES, vt.shape[1]), 0) == 0,
                          1.0, 0.0).astype(BF16)
    for hd in range(MLA_HEADS):
        vt_ref[hd * VT_ROWS:hd * VT_ROWS + MLA_V, :] = vt[hd * MLA_V:(hd + 1) * MLA_V]
        vt_ref[hd * VT_ROWS + MLA_V:(hd + 1) * VT_ROWS, :] = ones_rows

    qk_r = _dot(h, wal_ref[:, OFF_QK_R:OFF_V_R])
    gate_ref[:, D_MODEL:] = _dot(h, wal_ref[:, OFF_GATE + D_MODEL:OFF_END]).astype(BF16)
    k_off = RET_HEADS * RET_QK
    for j in range(k_off // LANES):
        qr_ref[:, j * LANES:(j + 1) * LANES] = rope_r(qk_r[:, j * LANES:(j + 1) * LANES]).astype(BF16)
        kb = qk_r[:, k_off + j * LANES:k_off + (j + 1) * LANES]
        kr_ref[:, j * LANES:(j + 1) * LANES] = (rope_r(kb) * (RET_QK ** -0.5)).astype(BF16)


def _proj_call(x2, pos2, gmix, wsm, win, gqa, wqb, gkva, wkvb, gqn, gknn, gknr, inv):
    t = x2.shape[0]
    tm = TM_PROJ
    row = lambda w: pl.BlockSpec((tm, w), lambda i: (i, 0))
    out_shape = (
        jax.ShapeDtypeStruct((t, MLA_HEADS * LANES), BF16),
        jax.ShapeDtypeStruct((t, MLA_HEADS * LANES), BF16),
        jax.ShapeDtypeStruct((MLA_HEADS * VT_ROWS, t), BF16),
        jax.ShapeDtypeStruct((t, RET_HEADS * RET_QK), BF16),
        jax.ShapeDtypeStruct((t, RET_HEADS * RET_QK), BF16),
        jax.ShapeDtypeStruct((t, RET_HEADS * RET_V), BF16),
        jax.ShapeDtypeStruct((t, RET_HEADS * RET_V), BF16),
        jax.ShapeDtypeStruct((t, 2 * D_MODEL), BF16),
    )
    return pl.pallas_call(
        _proj_kernel,
        grid=(t // tm,),
        in_specs=[
            row(D_MODEL), row(1),
            _const_spec(gmix.shape), _const_spec(wsm.shape), _const_spec(win.shape),
            _const_spec(gqa.shape),
            _const_spec(wqb.shape), _const_spec(gkva.shape), _const_spec(wkvb.shape),
            _const_spec(gqn.shape), _const_spec(gknn.shape), _const_spec(gknr.shape),
            _const_spec(inv.shape),
        ],
        out_specs=(
            row(MLA_HEADS * LANES), row(MLA_HEADS * LANES),
            pl.BlockSpec((MLA_HEADS * VT_ROWS, tm), lambda i: (0, i)),
            row(RET_HEADS * RET_QK), row(RET_HEADS * RET_QK),
            row(RET_HEADS * RET_V), row(RET_HEADS * RET_V), row(2 * D_MODEL),
        ),
        out_shape=out_shape,
        scratch_shapes=[pltpu.VMEM((D_MODEL, OFF_END), BF16)],
        compiler_params=pltpu.CompilerParams(
            dimension_semantics=("arbitrary",), vmem_limit_bytes=VMEM_LIMIT_BYTES),
        name="proj_in",
    )(x2, pos2, gmix, wsm, win, gqa, wqb, gkva, wkvb, gqn, gknn, gknr, inv)


def _attn_kernel(q_ref, k_ref, vt_ref, *refs, seq, n_w):
    o_ref, s_ref = refs[n_w], refs[-1]
    for w_ref, w_bf_ref in zip(refs[:n_w], refs[n_w + 1:-1]):
        w_bf_ref[...] = w_ref[...].astype(BF16)

    n_q = seq // TQ
    n_k = seq // TK
    assert n_k % S_SLOTS == 0 and n_k > S_AHEAD

    def q_transposed(qi):
        q0 = pl.multiple_of(qi * TQ, TQ)
        return q_ref[pl.ds(q0, TQ), :].astype(F32).T.astype(BF16)

    def scores(qt, ki):
        s = _dot(k_ref[ki * TK:(ki + 1) * TK, :], qt)
        s_ref[ki % S_SLOTS] = s
        return jnp.max(s, axis=0, keepdims=True)

    def q_tile(qi, carry):
        qt, cms = carry
        qt_next = q_transposed(jnp.minimum(qi + 1, n_q - 1))
        m, cms = cms[0], cms[1:]
        acc = jnp.zeros((VT_ROWS, TQ), F32)
        for ki in range(n_k):
            ka = ki + S_AHEAD
            cms = cms + (scores(qt, ka) if ka < n_k else scores(qt_next, ka - n_k),)
            p = jnp.exp2(s_ref[ki % S_SLOTS] - m).astype(BF16)
            acc = acc + _dot(vt_ref[:, ki * TK:(ki + 1) * TK], p)
            if ki + 1 < n_k:
                m_new = jnp.maximum(m, cms[0])
                acc = acc * jnp.exp2(m - m_new)
                m, cms = m_new, cms[1:]
        q0 = pl.multiple_of(qi * TQ, TQ)
        o_ref[:, pl.ds(q0, TQ)] = (acc[:MLA_V] / acc[MLA_V:MLA_V + 1]).astype(BF16)
        return qt_next, cms

    def q_tiles(j, carry):
        for u in range(Q_PER_BODY):
            carry = q_tile(j * Q_PER_BODY + u, carry)
        return carry

    assert n_q % Q_PER_BODY == 0
    qt0 = q_transposed(0)
    lax.fori_loop(0, n_q // Q_PER_BODY, q_tiles,
                  (qt0, tuple(scores(qt0, i) for i in range(S_AHEAD))))


def _attn_call(q, k, vt, weights, batch, seq):
    t = batch * seq
    steps = batch * MLA_HEADS
    slabs = []
    for w in weights:
        rows = w.shape[0] // steps
        assert rows * steps == w.shape[0] and rows % BF16_SUBLANES == 0
        slabs.append(pl.BlockSpec((rows, w.shape[1]), lambda b, h: (b * MLA_HEADS + h, 0)))
    out = pl.pallas_call(
        functools.partial(_attn_kernel, seq=seq, n_w=len(weights)),
        grid=(batch, MLA_HEADS),
        in_specs=[
            pl.BlockSpec((seq, LANES), lambda b, h: (b, h)),
            pl.BlockSpec((seq, LANES), lambda b, h: (b, h)),
            pl.BlockSpec((VT_ROWS, seq), lambda b, h: (h, b)),
            *slabs,
        ],
        out_specs=(pl.BlockSpec((MLA_V, seq), lambda b, h: (h, b)), *slabs),
        out_shape=(jax.ShapeDtypeStruct((MLA_HEADS * MLA_V, t), BF16),
                   *(jax.ShapeDtypeStruct(w.shape, BF16) for w in weights)),
        scratch_shapes=[pltpu.VMEM((S_SLOTS, TK, TQ), F32)],
        compiler_params=pltpu.CompilerParams(
            dimension_semantics=("arbitrary", "arbitrary"), vmem_limit_bytes=VMEM_LIMIT_BYTES),
        name="mla_attention",
    )(q, k, vt, *weights)
    return out[0], out[1:]


def _ret_kernel(dec_ref, q_ref, k_ref, v_ref, o_ref, kv_ref, st_ref, a_ref, *, seq):
    c = RET_C
    n = seq // c
    hd = pl.program_id(1)
    lo = (hd % 2) * RET_QK
    lane = lax.broadcasted_iota(jnp.int32, (1, LANES), 1)
    own = (lane >= lo) & (lane < lo + RET_QK)
    row = lax.broadcasted_iota(jnp.int32, (LANES, 1), 0)
    own_row = (row >= lo) & (row < lo + RET_QK)
    own_bf = jnp.where(own, 1.0, 0.0).astype(BF16)

    lg_f = -jnp.exp(jnp.full((1, 1), dec_ref[0, hd], F32))
    lg_b = -jnp.exp(jnp.full((1, 1), dec_ref[1, hd], F32))
    a = lax.broadcasted_iota(jnp.int32, (c, 1), 0).astype(F32)
    b = lax.broadcasted_iota(jnp.int32, (1, c), 1).astype(F32)
    q_decay = jnp.where(own, jnp.exp(lg_f * (a + 1.0)), jnp.exp(lg_b * (c - a)))
    k_decay = jnp.where(own, jnp.exp(lg_f * (c - 1.0 - a)), jnp.exp(lg_b * a))
    cd_f = jnp.exp(lg_f * float(c))
    cd_b = jnp.exp(lg_b * float(c))
    diff = a - b
    decay = jnp.where(diff >= 0, jnp.exp(lg_f * jnp.maximum(diff, 0.0)),
                      jnp.exp(lg_b * jnp.maximum(-diff, 0.0)))

    def both_halves(v):
        return jnp.where(own, v, pltpu.roll(v, RET_QK, 1))

    def chunk_kv(i, carry):
        r0 = pl.multiple_of(i * c, c)
        k = k_ref[pl.ds(r0, c), :].astype(F32)
        kv_ref[i] = _dot((both_halves(k) * k_decay).T.astype(BF16), v_ref[pl.ds(r0, c), :])
        return carry

    lax.fori_loop(0, n, chunk_kv, 0, unroll=True)

    def fwd_scan(i, f):
        st_ref[i] = f.astype(BF16)
        return f * cd_f + kv_ref[i]

    lax.fori_loop(0, n, fwd_scan, jnp.zeros((LANES, RET_V), F32), unroll=True)

    def bwd_scan(j, s):
        i = n - 1 - j
        st_ref[i] = jnp.where(own_row, st_ref[i].astype(F32), s).astype(BF16)
        return s * cd_b + kv_ref[i]

    lax.fori_loop(0, n, bwd_scan, jnp.zeros((LANES, RET_V), F32), unroll=True)

    def scores(i):
        rows = slice(i * c, (i + 1) * c)
        a_ref[i % RET_SLOTS] = lax.dot_general(
            q_ref[rows, :] * own_bf, k_ref[rows, :], (((1,), (1,)), ((), ())),
            preferred_element_type=F32)

    for i in range(min(RET_AHEAD, n)):
        scores(i)
    for i in range(n):
        if i + RET_AHEAD < n:
            scores(i + RET_AHEAD)
        rows = slice(i * c, (i + 1) * c)
        p = (a_ref[i % RET_SLOTS] * decay).astype(BF16)
        qd = (both_halves(q_ref[rows, :].astype(F32)) * q_decay).astype(BF16)
        o = _dot(jnp.concatenate([p, qd], axis=1),
                 jnp.concatenate([v_ref[rows, :], st_ref[i]], axis=0))
        o_ref[rows, :] = (o * _rms_scale(o, RET_V)).astype(BF16)


def _ret_call(dec, qr, kr, vr, batch, seq):
    t = batch * seq
    pair = pl.BlockSpec((seq, LANES), lambda b, h: (b, h // 2))
    head = pl.BlockSpec((seq, LANES), lambda b, h: (b, h))
    return pl.pallas_call(
        functools.partial(_ret_kernel, seq=seq),
        grid=(batch, RET_HEADS),
        in_specs=[pl.BlockSpec(memory_space=pltpu.SMEM), pair, pair, head],
        out_specs=head,
        out_shape=jax.ShapeDtypeStruct((t, RET_HEADS * RET_V), BF16),
        scratch_shapes=[pltpu.VMEM((seq // RET_C, LANES, RET_V), F32),
                        pltpu.VMEM((seq // RET_C, LANES, RET_V), BF16),
                        pltpu.VMEM((RET_SLOTS, RET_C, RET_C), F32)],
        compiler_params=pltpu.CompilerParams(
            dimension_semantics=("arbitrary", "arbitrary"), vmem_limit_bytes=VMEM_LIMIT_BYTES),
        name="retention",
    )(dec, qr, kr, vr)


def _out_kernel(x_ref, ot_ref, ret_ref, gr_ref, gate_ref, wa_ref, wb_ref, wo_ref, gffn_ref,
                wgu_ref, wd_ref, y_ref, act_ref):
    g = gr_ref[...].astype(F32)
    o_b = (g * jax.nn.sigmoid(g) * ret_ref[...].astype(F32)).astype(BF16)
    y_b = _dot(o_b, wb_ref[...])
    o_a = ot_ref[...].astype(F32).T.astype(BF16)
    y_a = _dot(o_a, wa_ref[...])
    merged = (jax.nn.sigmoid(gate_ref[:, :D_MODEL].astype(F32)) * y_a
              + jax.nn.sigmoid(gate_ref[:, D_MODEL:].astype(F32)) * y_b).astype(BF16)
    x1 = x_ref[...] + _dot(merged, wo_ref[...])
    h2 = (x1 * _rms_scale(x1, D_MODEL) * gffn_ref[...]).astype(BF16)
    for cidx in range(FFN_HIDDEN // FFN_CHUNK):
        c0 = cidx * FFN_CHUNK
        gate = _dot(h2, wgu_ref[:, c0:c0 + FFN_CHUNK])
        up = _dot(h2, wgu_ref[:, FFN_HIDDEN + c0:FFN_HIDDEN + c0 + FFN_CHUNK])
        act_ref[:, c0:c0 + FFN_CHUNK] = (gate * jax.nn.sigmoid(gate) * up).astype(BF16)
    y_ref[...] = x1 + _dot(act_ref[...], wd_ref[...])


def _out_call(x2, ot, ret, gr, gates, wa, wb, wo, gffn, wgu, wd):
    t = x2.shape[0]
    tm = TM_OUT
    row = lambda w: pl.BlockSpec((tm, w), lambda i: (i, 0))
    return pl.pallas_call(
        _out_kernel,
        grid=(t // tm,),
        in_specs=[
            row(D_MODEL),
            pl.BlockSpec((MLA_HEADS * MLA_V, tm), lambda i: (0, i)),
            row(RET_HEADS * RET_V), row(RET_HEADS * RET_V), row(2 * D_MODEL),
            _const_spec(wa.shape), _const_spec(wb.shape), _const_spec(wo.shape),
            _const_spec(gffn.shape), _const_spec(wgu.shape), _const_spec(wd.shape),
        ],
        out_specs=row(D_MODEL),
        out_shape=jax.ShapeDtypeStruct((t, D_MODEL), F32),
        scratch_shapes=[pltpu.VMEM((tm, FFN_HIDDEN), BF16)],
        compiler_params=pltpu.CompilerParams(
            dimension_semantics=("arbitrary",), vmem_limit_bytes=VMEM_LIMIT_BYTES),
        name="merge_ffn",
    )(x2, ot, ret, gr, gates, wa, wb, wo, gffn, wgu, wd)


def _mla_head_lane_map():
    half = MLA_ROPE // 2
    src = np.full(LANES, MLA_QK, np.int32)
    for i in range(half):
        src[2 * i] = MLA_NOPE + i
        src[RET_QK // 2 + 2 * i] = MLA_NOPE + half + i
    for n in range(MLA_NOPE // 2):
        src[2 * n + 1] = n
        src[MLA_NOPE + n] = MLA_NOPE // 2 + n
    return src


def _take_lanes(w, src, n_valid):
    pad = [(0, 0)] * (w.ndim - 1) + [(0, 1)]
    return jnp.take(jnp.pad(w, pad), np.minimum(src, n_valid), axis=-1)


def _layout_w_in(w_in, src):
    latent = MLA_Q_RANK + MLA_KV_RANK
    rope_src = np.where(src >= MLA_NOPE, src - MLA_NOPE, MLA_ROPE)
    k_rope_blk = _take_lanes(w_in[:, latent:latent + MLA_ROPE], rope_src, MLA_ROPE)
    small = jnp.concatenate([w_in[:, :latent], k_rope_blk], axis=1).astype(BF16)
    return small, w_in.astype(BF16)


def _layout_w_q_b(w_q_b, src):
    w = w_q_b.reshape(MLA_Q_RANK, MLA_HEADS, MLA_QK)
    return _take_lanes(w, src, MLA_QK).reshape(MLA_Q_RANK, MLA_HEADS * LANES).astype(BF16)


def _layout_w_kv_b(w_kv_b, src):
    w = w_kv_b.reshape(MLA_KV_RANK, MLA_HEADS, MLA_NOPE + MLA_V)
    k_nope = _take_lanes(w[:, :, :MLA_NOPE], src, MLA_NOPE)
    v = w[:, :, MLA_NOPE:]
    return jnp.concatenate([k_nope.reshape(MLA_KV_RANK, MLA_HEADS * LANES),
                            v.reshape(MLA_KV_RANK, MLA_HEADS * MLA_V)], axis=1).astype(BF16)


def _rope_freqs():
    half = RET_QK // 2
    inv = ROPE_THETA ** (-(np.arange(LANES) % half) / half)
    return jnp.asarray(inv.reshape(1, LANES), F32)


def kernel(x, positions, g_mix, w_in, g_q_a, w_q_b, g_kv_a, w_kv_b, g_qn, g_kn, w_mla_out,
           ret_decay_fwd, ret_decay_bwd, w_ret_out, w_out, g_ffn, w_gate_up, w_down):
    batch, seq, d = x.shape
    t = batch * seq
    x2 = x.reshape(t, d)
    pos2 = positions.reshape(t, 1).astype(jnp.int32)
    inv = _rope_freqs()

    row = lambda v: v.astype(F32).reshape(1, -1)
    src = _mla_head_lane_map()
    gqn = _take_lanes(row(g_qn), src, MLA_QK) * (MLA_QK ** -0.5 * math.log2(math.e))
    gkn = _take_lanes(row(g_kn), src, MLA_QK)
    gknn = jnp.where(src < MLA_NOPE, gkn, 0.0)
    gknr = jnp.where(src >= MLA_NOPE, gkn, 0.0)

    q, k, vt, qr, kr, vr, gr, gates = _proj_call(
        x2, pos2, row(g_mix), *_layout_w_in(w_in, src), row(g_q_a), _layout_w_q_b(w_q_b, src),
        row(g_kv_a), _layout_w_kv_b(w_kv_b, src), gqn, gknn, gknr, inv)

    ot, (wa, wb, wo, wgu, wd) = _attn_call(
        q, k, vt, (w_mla_out, w_ret_out, w_out, w_gate_up, w_down), batch, seq)
    dec = jnp.stack([ret_decay_fwd, ret_decay_bwd]).astype(F32)
    ret = _ret_call(dec, qr, kr, vr, batch, seq)

    y = _out_call(x2, ot, ret, gr, gates, wa, wb, wo, row(g_ffn), wgu, wd)
    return y.reshape(batch, seq, d)
```

```python
import functools
import math

import numpy as np
import jax
import jax.numpy as jnp
from jax import lax
from jax.experimental import pallas as pl
from jax.experimental.pallas import tpu as pltpu

F32 = jnp.float32
BF16 = jnp.bfloat16

LANES = 128
VMEM_LIMIT_BYTES = 56 * 1024 * 1024

D_MODEL = 1024
MLA_HEADS = 8
MLA_Q_RANK = 256
MLA_KV_RANK = 128
MLA_NOPE = 64
MLA_ROPE = 32
MLA_V = 64
MLA_QK = MLA_NOPE + MLA_ROPE
RET_HEADS = 8
RET_QK = 64
RET_V = 128
FFN_HIDDEN = 2816
ROPE_THETA = 10000.0
EPS = 1e-6

W_IN_MLA_COLS = MLA_Q_RANK + MLA_KV_RANK + MLA_ROPE
OFF_QK_R = 0
OFF_V_R = OFF_QK_R + 2 * RET_HEADS * RET_QK
OFF_G_R = OFF_V_R + RET_HEADS * RET_V
OFF_GATE = OFF_G_R + RET_HEADS * RET_V
OFF_END = OFF_GATE + 2 * D_MODEL

TM_PROJ = 512
TM_OUT = 512
TQ = 512
TK = 256
S_AHEAD = 2
S_SLOTS = 4
Q_PER_BODY = 4
BF16_SUBLANES = 16
VT_ROWS = MLA_V + BF16_SUBLANES
RET_C = 256
RET_AHEAD = 2
RET_SLOTS = 4
FFN_CHUNK = 256


def _dot(a, b):
    return jnp.dot(a, b, preferred_element_type=F32)


def _rms_scale(v, n):
    return lax.rsqrt(jnp.sum(v * v, axis=-1, keepdims=True) * (1.0 / n) + EPS)


def _const_spec(shape):
    zeros = (0,) * len(shape)
    return pl.BlockSpec(shape, lambda *_: zeros, pipeline_mode=pl.Buffered(1))


def _proj_kernel(x_ref, pos_ref, gmix_ref, wsm_ref, win_ref, wtail_ref, gqa_ref, wqb_ref, gkva_ref,
                 wkvb_ref, gqn_ref, gknn_ref, gknr_ref, inv_ref,
                 q_ref, k_ref, vt_ref, qr_ref, kr_ref, vr_ref, gr_ref, gate_ref, wal_ref):
    @pl.when(pl.program_id(0) == 0)
    def _():
        main = win_ref.shape[1]
        for c0 in range(W_IN_MLA_COLS, main, LANES):
            width = min(LANES, main - c0)
            wal_ref[:, c0 - W_IN_MLA_COLS:c0 - W_IN_MLA_COLS + width] = win_ref[:, c0:c0 + width]
        wal_ref[:, main - W_IN_MLA_COLS:] = wtail_ref[...]

    x = x_ref[...]
    h = (x * _rms_scale(x, D_MODEL) * gmix_ref[...]).astype(BF16)

    pos = pos_ref[...].astype(F32)
    lane = lax.broadcasted_iota(jnp.int32, (1, LANES), 1)
    half = RET_QK // 2
    ang = pos * inv_ref[...]
    cs_r = jnp.cos(ang)
    sn = jnp.sin(ang)
    first_half = (lane % RET_QK) < half
    lo_r = jnp.where(first_half, -sn, 0.0)
    hi_r = jnp.where(first_half, 0.0, sn)
    mla_rope = ((lane % 2) == 0) & (lane < RET_QK)
    cs_m = jnp.where(mla_rope, cs_r, 1.0)
    lo_m = jnp.where(mla_rope, lo_r, 0.0)
    hi_m = jnp.where(mla_rope, hi_r, 0.0)

    def rope(v, cs, lo, hi):
        return (v * cs + pltpu.roll(v, LANES - half, 1) * lo + pltpu.roll(v, half, 1) * hi)

    rope_m = functools.partial(rope, cs=cs_m, lo=lo_m, hi=hi_m)
    rope_r = functools.partial(rope, cs=cs_r, lo=lo_r, hi=hi_r)

    small = _dot(h, wsm_ref[...])
    cq = small[:, :MLA_Q_RANK]
    ckv = small[:, MLA_Q_RANK:MLA_Q_RANK + MLA_KV_RANK]
    krb = small[:, MLA_Q_RANK + MLA_KV_RANK:]

    cqn = (cq * _rms_scale(cq, MLA_Q_RANK) * gqa_ref[...]).astype(BF16)
    qf = _dot(cqn, wqb_ref[...])
    gqn = gqn_ref[...]
    gate_ref[:, :D_MODEL] = _dot(h, wal_ref[:, OFF_GATE:OFF_GATE + D_MODEL]).astype(BF16)
    for hd in range(MLA_HEADS):
        blk = qf[:, hd * LANES:(hd + 1) * LANES]
        qn = blk * _rms_scale(blk, MLA_QK) * gqn
        q_ref[:, hd * LANES:(hd + 1) * LANES] = rope_m(qn).astype(BF16)

    ckvn = (ckv * _rms_scale(ckv, MLA_KV_RANK) * gkva_ref[...]).astype(BF16)
    kvf = _dot(ckvn, wkvb_ref[...])
    vr_ref[...] = _dot(h, wal_ref[:, OFF_V_R:OFF_G_R]).astype(BF16)
    kr_roped = rope_m(krb * gknr_ref[...])
    ss_rope = jnp.sum(krb * krb, axis=-1, keepdims=True)
    gknn = gknn_ref[...]
    for hd in range(MLA_HEADS):
        blk = kvf[:, hd * LANES:(hd + 1) * LANES]
        ss = jnp.sum(blk * blk, axis=-1, keepdims=True) + ss_rope
        r = lax.rsqrt(ss * (1.0 / MLA_QK) + EPS)
        k_ref[:, hd * LANES:(hd + 1) * LANES] = ((blk * gknn + kr_roped) * r).astype(BF16)
    gr_ref[...] = _dot(h, wal_ref[:, OFF_G_R:OFF_GATE]).astype(BF16)
    vt = kvf[:, MLA_HEADS * LANES:].T.astype(BF16)
    ones_rows = jnp.where(lax.broadcasted_iota(jnp.int32, (BF16_SUBLANES, vt.shape[1]), 0) == 0,
                          1.0, 0.0).astype(BF16)
    for hd in range(MLA_HEADS):
        vt_ref[hd * VT_ROWS:hd * VT_ROWS + MLA_V, :] = vt[hd * MLA_V:(hd + 1) * MLA_V]
        vt_ref[hd * VT_ROWS + MLA_V:(hd + 1) * VT_ROWS, :] = ones_rows

    qk_r = _dot(h, wal_ref[:, OFF_QK_R:OFF_V_R])
    gate_ref[:, D_MODEL:] = _dot(h, wal_ref[:, OFF_GATE + D_MODEL:OFF_END]).astype(BF16)
    k_off = RET_HEADS * RET_QK
    for j in range(k_off // LANES):
        qr_ref[:, j * LANES:(j + 1) * LANES] = rope_r(qk_r[:, j * LANES:(j + 1) * LANES]).astype(BF16)
        kb = qk_r[:, k_off + j * LANES:k_off + (j + 1) * LANES]
        kr_ref[:, j * LANES:(j + 1) * LANES] = (rope_r(kb) * (RET_QK ** -0.5)).astype(BF16)


def _proj_call(x2, pos2, gmix, wsm, win, wtail, gqa, wqb, gkva, wkvb, gqn, gknn, gknr, inv):
    t = x2.shape[0]
    tm = TM_PROJ
    row = lambda w: pl.BlockSpec((tm, w), lambda i: (i, 0))
    out_shape = (
        jax.ShapeDtypeStruct((t, MLA_HEADS * LANES), BF16),
        jax.ShapeDtypeStruct((t, MLA_HEADS * LANES), BF16),
        jax.ShapeDtypeStruct((MLA_HEADS * VT_ROWS, t), BF16),
        jax.ShapeDtypeStruct((t, RET_HEADS * RET_QK), BF16),
        jax.ShapeDtypeStruct((t, RET_HEADS * RET_QK), BF16),
        jax.ShapeDtypeStruct((t, RET_HEADS * RET_V), BF16),
        jax.ShapeDtypeStruct((t, RET_HEADS * RET_V), BF16),
        jax.ShapeDtypeStruct((t, 2 * D_MODEL), BF16),
    )
    return pl.pallas_call(
        _proj_kernel,
        grid=(t // tm,),
        in_specs=[
            row(D_MODEL), row(1),
            _const_spec(gmix.shape), _const_spec(wsm.shape), _const_spec(win.shape),
            _const_spec(wtail.shape), _const_spec(gqa.shape),
            _const_spec(wqb.shape), _const_spec(gkva.shape), _const_spec(wkvb.shape),
            _const_spec(gqn.shape), _const_spec(gknn.shape), _const_spec(gknr.shape),
            _const_spec(inv.shape),
        ],
        out_specs=(
            row(MLA_HEADS * LANES), row(MLA_HEADS * LANES),
            pl.BlockSpec((MLA_HEADS * VT_ROWS, tm), lambda i: (0, i)),
            row(RET_HEADS * RET_QK), row(RET_HEADS * RET_QK),
            row(RET_HEADS * RET_V), row(RET_HEADS * RET_V), row(2 * D_MODEL),
        ),
        out_shape=out_shape,
        scratch_shapes=[pltpu.VMEM((D_MODEL, OFF_END), BF16)],
        compiler_params=pltpu.CompilerParams(
            dimension_semantics=("arbitrary",), vmem_limit_bytes=VMEM_LIMIT_BYTES),
        name="proj_in",
    )(x2, pos2, gmix, wsm, win, wtail, gqa, wqb, gkva, wkvb, gqn, gknn, gknr, inv)


def _attn_kernel(q_ref, k_ref, vt_ref, *refs, seq, n_w):
    o_ref, s_ref = refs[n_w], refs[-1]
    for w_ref, w_bf_ref in zip(refs[:n_w], refs[n_w + 1:-1]):
        w_bf_ref[...] = w_ref[...].astype(BF16)

    n_q = seq // TQ
    n_k = seq // TK
    assert n_k % S_SLOTS == 0 and n_k > S_AHEAD

    def q_transposed(qi):
        q0 = pl.multiple_of(qi * TQ, TQ)
        return q_ref[pl.ds(q0, TQ), :].astype(F32).T.astype(BF16)

    def scores(qt, ki):
        s = _dot(k_ref[ki * TK:(ki + 1) * TK, :], qt)
        s_ref[ki % S_SLOTS] = s
        return jnp.max(s, axis=0, keepdims=True)

    def q_tile(qi, carry):
        qt, cms = carry
        qt_next = q_transposed(jnp.minimum(qi + 1, n_q - 1))
        m, cms = cms[0], cms[1:]
        acc = jnp.zeros((VT_ROWS, TQ), F32)
        for ki in range(n_k):
            ka = ki + S_AHEAD
            cms = cms + (scores(qt, ka) if ka < n_k else scores(qt_next, ka - n_k),)
            p = jnp.exp2(s_ref[ki % S_SLOTS] - m).astype(BF16)
            acc = acc + _dot(vt_ref[:, ki * TK:(ki + 1) * TK], p)
            if ki + 1 < n_k:
                m_new = jnp.maximum(m, cms[0])
                acc = acc * jnp.exp2(m - m_new)
                m, cms = m_new, cms[1:]
        q0 = pl.multiple_of(qi * TQ, TQ)
        o_ref[:, pl.ds(q0, TQ)] = (acc[:MLA_V] / acc[MLA_V:MLA_V + 1]).astype(BF16)
        return qt_next, cms

    def q_tiles(j, carry):
        for u in range(Q_PER_BODY):
            carry = q_tile(j * Q_PER_BODY + u, carry)
        return carry

    assert n_q % Q_PER_BODY == 0
    qt0 = q_transposed(0)
    lax.fori_loop(0, n_q // Q_PER_BODY, q_tiles,
                  (qt0, tuple(scores(qt0, i) for i in range(S_AHEAD))))


def _attn_call(q, k, vt, weights, batch, seq):
    t = batch * seq
    steps = batch * MLA_HEADS
    slabs = []
    for w in weights:
        rows = w.shape[0] // steps
        assert rows * steps == w.shape[0] and rows % BF16_SUBLANES == 0
        slabs.append(pl.BlockSpec((rows, w.shape[1]), lambda b, h: (b * MLA_HEADS + h, 0)))
    out = pl.pallas_call(
        functools.partial(_attn_kernel, seq=seq, n_w=len(weights)),
        grid=(batch, MLA_HEADS),
        in_specs=[
            pl.BlockSpec((seq, LANES), lambda b, h: (b, h)),
            pl.BlockSpec((seq, LANES), lambda b, h: (b, h)),
            pl.BlockSpec((VT_ROWS, seq), lambda b, h: (h, b)),
            *slabs,
        ],
        out_specs=(pl.BlockSpec((MLA_V, seq), lambda b, h: (h, b)), *slabs),
        out_shape=(jax.ShapeDtypeStruct((MLA_HEADS * MLA_V, t), BF16),
                   *(jax.ShapeDtypeStruct(w.shape, BF16) for w in weights)),
        scratch_shapes=[pltpu.VMEM((S_SLOTS, TK, TQ), F32)],
        compiler_params=pltpu.CompilerParams(
            dimension_semantics=("arbitrary", "arbitrary"), vmem_limit_bytes=VMEM_LIMIT_BYTES),
        name="mla_attention",
    )(q, k, vt, *weights)
    return out[0], out[1:]


def _ret_kernel(dec_ref, q_ref, k_ref, v_ref, o_ref, kv_ref, st_ref, a_ref, *, seq):
    c = RET_C
    n = seq // c
    hd = pl.program_id(1)
    lo = (hd % 2) * RET_QK
    lane = lax.broadcasted_iota(jnp.int32, (1, LANES), 1)
    own = (lane >= lo) & (lane < lo + RET_QK)
    row = lax.broadcasted_iota(jnp.int32, (LANES, 1), 0)
    own_row = (row >= lo) & (row < lo + RET_QK)
    own_bf = jnp.where(own, 1.0, 0.0).astype(BF16)

    lg_f = -jnp.exp(jnp.full((1, 1), dec_ref[0, hd], F32))
    lg_b = -jnp.exp(jnp.full((1, 1), dec_ref[1, hd], F32))
    a = lax.broadcasted_iota(jnp.int32, (c, 1), 0).astype(F32)
    b = lax.broadcasted_iota(jnp.int32, (1, c), 1).astype(F32)
    q_decay = jnp.where(own, jnp.exp(lg_f * (a + 1.0)), jnp.exp(lg_b * (c - a)))
    k_decay = jnp.where(own, jnp.exp(lg_f * (c - 1.0 - a)), jnp.exp(lg_b * a))
    cd_f = jnp.exp(lg_f * float(c))
    cd_b = jnp.exp(lg_b * float(c))
    diff = a - b
    decay = jnp.where(diff >= 0, jnp.exp(lg_f * jnp.maximum(diff, 0.0)),
                      jnp.exp(lg_b * jnp.maximum(-diff, 0.0)))

    def both_halves(v):
        return jnp.where(own, v, pltpu.roll(v, RET_QK, 1))

    def chunk_kv(i, carry):
        r0 = pl.multiple_of(i * c, c)
        k = k_ref[pl.ds(r0, c), :].astype(F32)
        kv_ref[i] = _dot((both_halves(k) * k_decay).T.astype(BF16), v_ref[pl.ds(r0, c), :])
        return carry

    lax.fori_loop(0, n, chunk_kv, 0, unroll=True)

    def fwd_scan(i, f):
        st_ref[i] = f.astype(BF16)
        return f * cd_f + kv_ref[i]

    lax.fori_loop(0, n, fwd_scan, jnp.zeros((LANES, RET_V), F32))

    def bwd_scan(j, s):
        i = n - 1 - j
        st_ref[i] = jnp.where(own_row, st_ref[i].astype(F32), s).astype(BF16)
        return s * cd_b + kv_ref[i]

    lax.fori_loop(0, n, bwd_scan, jnp.zeros((LANES, RET_V), F32))

    def scores(i):
        rows = slice(i * c, (i + 1) * c)
        a_ref[i % RET_SLOTS] = lax.dot_general(
            q_ref[rows, :] * own_bf, k_ref[rows, :], (((1,), (1,)), ((), ())),
            preferred_element_type=F32)

    for i in range(min(RET_AHEAD, n)):
        scores(i)
    for i in range(n):
        if i + RET_AHEAD < n:
            scores(i + RET_AHEAD)
        rows = slice(i * c, (i + 1) * c)
        p = (a_ref[i % RET_SLOTS] * decay).astype(BF16)
        qd = (both_halves(q_ref[rows, :].astype(F32)) * q_decay).astype(BF16)
        o = _dot(jnp.concatenate([p, qd], axis=1),
                 jnp.concatenate([v_ref[rows, :], st_ref[i]], axis=0))
        o_ref[rows, :] = (o * _rms_scale(o, RET_V)).astype(BF16)


def _ret_call(dec, qr, kr, vr, batch, seq):
    t = batch * seq
    pair = pl.BlockSpec((seq, LANES), lambda b, h: (b, h // 2))
    head = pl.BlockSpec((seq, LANES), lambda b, h: (b, h))
    return pl.pallas_call(
        functools.partial(_ret_kernel, seq=seq),
        grid=(batch, RET_HEADS),
        in_specs=[pl.BlockSpec(memory_space=pltpu.SMEM), pair, pair, head],
        out_specs=head,
        out_shape=jax.ShapeDtypeStruct((t, RET_HEADS * RET_V), BF16),
        scratch_shapes=[pltpu.VMEM((seq // RET_C, LANES, RET_V), F32),
                        pltpu.VMEM((seq // RET_C, LANES, RET_V), BF16),
                        pltpu.VMEM((RET_SLOTS, RET_C, RET_C), F32)],
        compiler_params=pltpu.CompilerParams(
            dimension_semantics=("arbitrary", "arbitrary"), vmem_limit_bytes=VMEM_LIMIT_BYTES),
        name="retention",
    )(dec, qr, kr, vr)


def _out_kernel(x_ref, ot_ref, ret_ref, gr_ref, gate_ref, wa_ref, wb_ref, wo_ref, gffn_ref,
                wgu_ref, wd_ref, y_ref, act_ref):
    o_a = ot_ref[...].astype(F32).T.astype(BF16)
    y_a = _dot(o_a, wa_ref[...])
    g = gr_ref[...].astype(F32)
    o_b = (g * jax.nn.sigmoid(g) * ret_ref[...].astype(F32)).astype(BF16)
    y_b = _dot(o_b, wb_ref[...])
    merged = (jax.nn.sigmoid(gate_ref[:, :D_MODEL].astype(F32)) * y_a
              + jax.nn.sigmoid(gate_ref[:, D_MODEL:].astype(F32)) * y_b).astype(BF16)
    x1 = x_ref[...] + _dot(merged, wo_ref[...])
    h2 = (x1 * _rms_scale(x1, D_MODEL) * gffn_ref[...]).astype(BF16)
    for cidx in range(FFN_HIDDEN // FFN_CHUNK):
        c0 = cidx * FFN_CHUNK
        gate = _dot(h2, wgu_ref[:, c0:c0 + FFN_CHUNK])
        up = _dot(h2, wgu_ref[:, FFN_HIDDEN + c0:FFN_HIDDEN + c0 + FFN_CHUNK])
        act_ref[:, c0:c0 + FFN_CHUNK] = (gate * jax.nn.sigmoid(gate) * up).astype(BF16)
    y_ref[...] = x1 + _dot(act_ref[...], wd_ref[...])


def _out_call(x2, ot, ret, gr, gates, wa, wb, wo, gffn, wgu, wd):
    t = x2.shape[0]
    tm = TM_OUT
    row = lambda w: pl.BlockSpec((tm, w), lambda i: (i, 0))
    return pl.pallas_call(
        _out_kernel,
        grid=(t // tm,),
        in_specs=[
            row(D_MODEL),
            pl.BlockSpec((MLA_HEADS * MLA_V, tm), lambda i: (0, i)),
            row(RET_HEADS * RET_V), row(RET_HEADS * RET_V), row(2 * D_MODEL),
            _const_spec(wa.shape), _const_spec(wb.shape), _const_spec(wo.shape),
            _const_spec(gffn.shape), _const_spec(wgu.shape), _const_spec(wd.shape),
        ],
        out_specs=row(D_MODEL),
        out_shape=jax.ShapeDtypeStruct((t, D_MODEL), F32),
        scratch_shapes=[pltpu.VMEM((tm, FFN_HIDDEN), BF16)],
        compiler_params=pltpu.CompilerParams(
            dimension_semantics=("arbitrary",), vmem_limit_bytes=VMEM_LIMIT_BYTES),
        name="merge_ffn",
    )(x2, ot, ret, gr, gates, wa, wb, wo, gffn, wgu, wd)


def _mla_head_lane_map():
    half = MLA_ROPE // 2
    src = np.full(LANES, MLA_QK, np.int32)
    for i in range(half):
        src[2 * i] = MLA_NOPE + i
        src[RET_QK // 2 + 2 * i] = MLA_NOPE + half + i
    for n in range(MLA_NOPE // 2):
        src[2 * n + 1] = n
        src[MLA_NOPE + n] = MLA_NOPE // 2 + n
    return src


def _take_lanes(w, src, n_valid):
    pad = [(0, 0)] * (w.ndim - 1) + [(0, 1)]
    return jnp.take(jnp.pad(w, pad), np.minimum(src, n_valid), axis=-1)


def _layout_w_in(w_in, src):
    latent = MLA_Q_RANK + MLA_KV_RANK
    rope_src = np.where(src >= MLA_NOPE, src - MLA_NOPE, MLA_ROPE)
    k_rope_blk = _take_lanes(w_in[:, latent:latent + MLA_ROPE], rope_src, MLA_ROPE)
    small = jnp.concatenate([w_in[:, :latent], k_rope_blk], axis=1).astype(BF16)
    main = w_in.shape[1] // LANES * LANES
    return small, w_in[:, :main].astype(BF16), w_in[:, main:].astype(BF16)


def _layout_w_q_b(w_q_b, src):
    w = w_q_b.reshape(MLA_Q_RANK, MLA_HEADS, MLA_QK)
    return _take_lanes(w, src, MLA_QK).reshape(MLA_Q_RANK, MLA_HEADS * LANES).astype(BF16)


def _layout_w_kv_b(w_kv_b, src):
    w = w_kv_b.reshape(MLA_KV_RANK, MLA_HEADS, MLA_NOPE + MLA_V)
    k_nope = _take_lanes(w[:, :, :MLA_NOPE], src, MLA_NOPE)
    v = w[:, :, MLA_NOPE:]
    return jnp.concatenate([k_nope.reshape(MLA_KV_RANK, MLA_HEADS * LANES),
                            v.reshape(MLA_KV_RANK, MLA_HEADS * MLA_V)], axis=1).astype(BF16)


def _rope_freqs():
    half = RET_QK // 2
    inv = ROPE_THETA ** (-(np.arange(LANES) % half) / half)
    return jnp.asarray(inv.reshape(1, LANES), F32)


def kernel(x, positions, g_mix, w_in, g_q_a, w_q_b, g_kv_a, w_kv_b, g_qn, g_kn, w_mla_out,
           ret_decay_fwd, ret_decay_bwd, w_ret_out, w_out, g_ffn, w_gate_up, w_down):
    batch, seq, d = x.shape
    t = batch * seq
    x2 = x.reshape(t, d)
    pos2 = positions.reshape(t, 1).astype(jnp.int32)
    inv = _rope_freqs()

    row = lambda v: v.astype(F32).reshape(1, -1)
    src = _mla_head_lane_map()
    gqn = _take_lanes(row(g_qn), src, MLA_QK) * (MLA_QK ** -0.5 * math.log2(math.e))
    gkn = _take_lanes(row(g_kn), src, MLA_QK)
    gknn = jnp.where(src < MLA_NOPE, gkn, 0.0)
    gknr = jnp.where(src >= MLA_NOPE, gkn, 0.0)

    q, k, vt, qr, kr, vr, gr, gates = _proj_call(
        x2, pos2, row(g_mix), *_layout_w_in(w_in, src), row(g_q_a), _layout_w_q_b(w_q_b, src),
        row(g_kv_a), _layout_w_kv_b(w_kv_b, src), gqn, gknn, gknr, inv)

    ot, (wa, wb, wo, wgu, wd) = _attn_call(
        q, k, vt, (w_mla_out, w_ret_out, w_out, w_gate_up, w_down), batch, seq)
    dec = jnp.stack([ret_decay_fwd, ret_decay_bwd]).astype(F32)
    ret = _ret_call(dec, qr, kr, vr, batch, seq)

    y = _out_call(x2, ot, ret, gr, gates, wa, wb, wo, row(g_ffn), wgu, wd)
    return y.reshape(batch, seq, d)
```
